```python
import math
import jax, jax.numpy as jnp
from jax import lax
import numpy as np

D_MODEL = 1024
BATCH = 2
SEQ = 8192
DEPTH = 4
DEC_BATCH = 32
DEC_SEQ = 4
PAST_LEN = 8192
PAGE_SIZE = 128

N_MIXERS = 2
N_CONV_LAYERS = (DEPTH + 1) // 2
N_NSA_LAYERS = DEPTH // 2
CONV_WIDTH = 31
N_HEADS = 16
HEAD_DIM = D_MODEL // N_HEADS
N_KV_HEADS = 4
GROUP = N_HEADS // N_KV_HEADS
ROT_DIM = HEAD_DIM // 4
ROPE_THETA = 500000.0
CMP_LEN = 32
CMP_STRIDE = 16
SEL_BLOCK = 64
N_SELECT = 16
N_LOCAL_SEL = 2
WINDOW = 512
Q_BLOCK = 128
Q_COLS = N_HEADS * HEAD_DIM
KV_COLS = N_KV_HEADS * HEAD_DIM
SPLIT_POINTS = tuple(Q_COLS + KV_COLS * i for i in range(7))
IN_COLS = Q_COLS + 6 * KV_COLS + 3 * N_HEADS
D_FF = 2816
N_EXPERTS = 8
TOP_K = 2
LN_EPS = 1e-5
ALPHA = (2 * DEPTH) ** 0.25
BETA = (8 * DEPTH) ** -0.25
NEG = -1e30
BIG = 1e30

kernel_name = 'hybrid_conformer_nsa_step'


def _layer_norm(x, g, b):
    xf = x.astype(jnp.float32)
    mu = jnp.mean(xf, -1, keepdims=True)
    var = jnp.mean(jnp.square(xf - mu), -1, keepdims=True)
    return ((xf - mu) * lax.rsqrt(var + LN_EPS) * g.astype(jnp.float32) + b.astype(jnp.float32)).astype(x.dtype)


def _rope(x, pos):
    half = ROT_DIM // 2
    inv_freq = ROPE_THETA ** (-jnp.arange(half, dtype=jnp.float32) / half)
    ang = pos.astype(jnp.float32)[:, None] * inv_freq[None, :]
    cos = jnp.cos(ang)[:, None, :]
    sin = jnp.sin(ang)[:, None, :]
    xr = x[..., :ROT_DIM].astype(jnp.float32)
    x1, x2 = xr[..., :half], xr[..., half:]
    rot = jnp.concatenate([x1 * cos - x2 * sin, x2 * cos + x1 * sin], axis=-1)
    return jnp.concatenate([rot.astype(x.dtype), x[..., ROT_DIM:]], axis=-1)


def _masked_softmax(s, mask):
    s = jnp.where(mask, s.astype(jnp.float32), NEG)
    m = jnp.max(s, -1, keepdims=True)
    p = jnp.exp(s - m) * mask
    return p / jnp.maximum(jnp.sum(p, -1, keepdims=True), 1e-30)


def _conv_mixer(x, prev_u, w_pw1, b_pw1, w_dw, b_dw, ln_g, ln_b, w_pw2, b_pw2):
    a, gt = jnp.split(x @ w_pw1 + b_pw1, 2, axis=-1)
    u = a * jax.nn.sigmoid(gt)
    u_ext = jnp.concatenate([prev_u.astype(u.dtype), u], axis=1)
    c = lax.conv_general_dilated(u_ext, w_dw[:, None, :], (1,), 'VALID',
                                 dimension_numbers=('NWC', 'WIO', 'NWC'),
                                 feature_group_count=D_MODEL) + b_dw
    c = _layer_norm(c, ln_g, ln_b)
    y = jax.nn.silu(c) @ w_pw2 + b_pw2
    return y, u_ext[:, -(CONV_WIDTH - 1):]


def _nsa_project(x, pos, w_in):
    B, T, _ = x.shape
    q, k_c, v_c, k_s, v_s, k_w, v_w, g = jnp.split(x @ w_in, SPLIT_POINTS, axis=-1)
    kv = lambda t: t.reshape(B, T, N_KV_HEADS, HEAD_DIM)
    q = _rope(q.reshape(B, T, N_HEADS, HEAD_DIM), pos)
    gates = jax.nn.sigmoid(g.astype(jnp.float32)).reshape(B, T, N_HEADS, 3).astype(x.dtype)
    return q, kv(k_c), kv(v_c), _rope(kv(k_s), pos), kv(v_s), _rope(kv(k_w), pos), kv(v_w), gates


def _compress(k, pe, w1, w2):
    B, T = k.shape[:2]
    lhs = k.transpose(0, 2, 1, 3).reshape(B * N_KV_HEADS, T, HEAD_DIM)
    h = lax.conv_general_dilated(lhs, w1, (CMP_STRIDE,), 'VALID',
                                 dimension_numbers=('NWC', 'WIO', 'NWC'))
    h = h + jnp.einsum('ld,lde->e', pe, w1)
    out = jax.nn.silu(h) @ w2
    n_cmp = out.shape[1]
    return out.reshape(B, N_KV_HEADS, n_cmp, HEAD_DIM).transpose(0, 2, 1, 3)


def _compressed_kv(k_c, v_c, pe, w1, w2):
    kc = _compress(k_c, pe[0], w1[0], w2[0])
    vc = _compress(v_c, pe[1], w1[1], w2[1])
    c_end = jnp.arange(kc.shape[1]) * CMP_STRIDE + (CMP_LEN - 1)
    return _rope(kc, c_end), vc, c_end


def _sel_blocks(k):
    B, T = k.shape[:2]
    n_blk = -(-T // SEL_BLOCK)
    k = jnp.pad(k, ((0, 0), (0, n_blk * SEL_BLOCK - T), (0, 0), (0, 0)))
    return k.reshape(B, n_blk, SEL_BLOCK, N_KV_HEADS, HEAD_DIM).transpose(0, 3, 1, 2, 4)


def _overlap(n_cmp, n_blk):
    cs = jnp.arange(n_cmp)[:, None] * CMP_STRIDE
    ss = jnp.arange(n_blk)[None, :] * SEL_BLOCK
    lo = jnp.maximum(cs, ss)
    hi = jnp.minimum(cs + CMP_LEN, ss + SEL_BLOCK)
    return jnp.maximum(hi - lo, 0).astype(jnp.float32) / CMP_LEN


def _nsa_attend(q, q_pos, kc, vc, c_end, ks_b, vs_b, kw, vw, w_pos, gates):
    B, Q = q.shape[:2]
    qg = q.reshape(B, Q, N_KV_HEADS, GROUP, HEAD_DIM) * (HEAD_DIM ** -0.5)
    s_c = jnp.einsum('bqkgd,bckd->bkgqc', qg, kc)
    p_c = _masked_softmax(s_c, c_end[None, :] <= q_pos[:, None])
    o_c = jnp.einsum('bkgqc,bckd->bqkgd', p_c.astype(vc.dtype), vc)
    n_blk = ks_b.shape[2]
    imp = jnp.einsum('bkgqc,cs->bkqs', p_c, _overlap(kc.shape[1], n_blk))
    blk = jnp.arange(n_blk)
    valid = blk[None, :] * SEL_BLOCK <= q_pos[:, None]
    lag = (q_pos // SEL_BLOCK)[:, None] - blk[None, :]
    forced = (blk[None, :] == 0) | ((lag >= 0) & (lag < N_LOCAL_SEL))
    score = jnp.where(forced, BIG, jnp.where(valid, imp, -BIG))
    n_sel = min(N_SELECT, n_blk)
    _, idx = lax.top_k(score, n_sel)
    bi = jnp.arange(B)[:, None, None, None]
    hi = jnp.arange(N_KV_HEADS)[None, :, None, None]
    k_g = ks_b[bi, hi, idx].reshape(B, N_KV_HEADS, Q, n_sel * SEL_BLOCK, HEAD_DIM)
    v_g = vs_b[bi, hi, idx].reshape(B, N_KV_HEADS, Q, n_sel * SEL_BLOCK, HEAD_DIM)
    key_pos = (idx[..., None] * SEL_BLOCK + jnp.arange(SEL_BLOCK)).reshape(B, N_KV_HEADS, Q, -1)
    m_s = (key_pos <= q_pos[None, None, :, None])[:, :, None]
    s_s = jnp.einsum('bqkgd,bkqld->bkgql', qg, k_g)
    p_s = _masked_softmax(s_s, m_s)
    o_s = jnp.einsum('bkgql,bkqld->bqkgd', p_s.astype(v_g.dtype), v_g)
    m_w = ((w_pos[None, :] <= q_pos[:, None]) & (w_pos[None, :] > q_pos[:, None] - WINDOW)
           & (w_pos[None, :] >= 0))
    s_w = jnp.einsum('bqkgd,blkd->bkgql', qg, kw)
    p_w = _masked_softmax(s_w, m_w)
    o_w = jnp.einsum('bkgql,blkd->bqkgd', p_w.astype(vw.dtype), vw)
    g = gates.reshape(B, Q, N_KV_HEADS, GROUP, 3)
    o = g[..., 0:1] * o_c + g[..., 1:2] * o_s + g[..., 2:3] * o_w
    return o.reshape(B, Q, N_HEADS * HEAD_DIM)


def _nsa_prompt(x, w_in, pe, w1, w2, w_o):
    B, T, _ = x.shape
    pos = jnp.arange(T)
    q, k_c, v_c, k_s, v_s, k_w, v_w, gates = _nsa_project(x, pos, w_in)
    kc, vc, c_end = _compressed_kv(k_c, v_c, pe, w1, w2)
    ks_b, vs_b = _sel_blocks(k_s), _sel_blocks(v_s)
    pad = ((0, 0), (WINDOW, 0), (0, 0), (0, 0))
    kw_pad, vw_pad = jnp.pad(k_w, pad), jnp.pad(v_w, pad)

    def block(n):
        s = n * Q_BLOCK
        qb = lax.dynamic_slice_in_dim(q, s, Q_BLOCK, axis=1)
        gb = lax.dynamic_slice_in_dim(gates, s, Q_BLOCK, axis=1)
        kwb = lax.dynamic_slice_in_dim(kw_pad, s, WINDOW + Q_BLOCK, axis=1)
        vwb = lax.dynamic_slice_in_dim(vw_pad, s, WINDOW + Q_BLOCK, axis=1)
        q_pos = s + jnp.arange(Q_BLOCK)
        w_pos = s - WINDOW + jnp.arange(WINDOW + Q_BLOCK)
        return _nsa_attend(qb, q_pos, kc, vc, c_end, ks_b, vs_b, kwb, vwb, w_pos, gb)

    o = lax.map(block, jnp.arange(T // Q_BLOCK))
    o = o.transpose(1, 0, 2, 3).reshape(B, T, N_HEADS * HEAD_DIM)
    w_buf = min(WINDOW, PAST_LEN)
    zpad = jnp.zeros((B, w_buf, N_KV_HEADS, HEAD_DIM), k_w.dtype)
    kw_state = jnp.concatenate([zpad, k_w], axis=1)[:, -w_buf:]
    vw_state = jnp.concatenate([zpad, v_w], axis=1)[:, -w_buf:]
    return o @ w_o, (k_c, v_c, k_s, v_s, kw_state, vw_state)


def _nsa_sample(x, pool_k_c, pool_v_c, pool_k_s, pool_v_s, kw_buf, vw_buf, page_table, w_in, pe, w1, w2, w_o):
    Bd, S, _ = x.shape
    P = page_table.shape[1] * PAGE_SIZE
    pos = P + jnp.arange(S)
    q, k_c, v_c, k_s, v_s, k_w, v_w, gates = _nsa_project(x, pos, w_in)
    gather = lambda pool: pool[page_table].reshape(Bd, P, N_KV_HEADS, HEAD_DIM)
    full = lambda pool, new: jnp.concatenate([gather(pool).astype(new.dtype), new], axis=1)
    kc, vc, c_end = _compressed_kv(full(pool_k_c, k_c), full(pool_v_c, v_c), pe, w1, w2)
    ks_b, vs_b = _sel_blocks(full(pool_k_s, k_s)), _sel_blocks(full(pool_v_s, v_s))
    w_buf = kw_buf.shape[1]
    kw = jnp.concatenate([kw_buf.astype(k_w.dtype), k_w], axis=1)
    vw = jnp.concatenate([vw_buf.astype(v_w.dtype), v_w], axis=1)
    w_pos = P - w_buf + jnp.arange(w_buf + S)
    o = _nsa_attend(q, pos, kc, vc, c_end, ks_b, vs_b, kw, vw, w_pos, gates)
    return o @ w_o, (k_c, v_c, k_s, v_s, kw[:, -w_buf:], vw[:, -w_buf:])


def _swiglu(x, w_gate, w_up, w_down):
    return (jax.nn.silu(x @ w_gate) * (x @ w_up)) @ w_down


def _moe(x, w_router, w_gate, w_up, w_down):
    logits = (x @ w_router).astype(jnp.float32)
    top_v, top_i = lax.top_k(logits, TOP_K)
    w = jax.nn.softmax(top_v, axis=-1)
    combine = jnp.sum(jax.nn.one_hot(top_i, N_EXPERTS, dtype=jnp.float32) * w[..., None], axis=-2)
    out = jnp.zeros_like(x)
    for e in range(N_EXPERTS):
        out = out + combine[..., e:e + 1].astype(x.dtype) * _swiglu(x, w_gate[e], w_up[e], w_down[e])
    return out


def setup_inputs(seed: int = 0) -> dict:
    key = jax.random.key(seed)
    ks = iter(jax.random.split(key, 40))
    nrm = lambda shape, scale: jax.random.normal(next(ks), shape, jnp.float32) * scale
    n_pages = PAST_LEN // PAGE_SIZE
    n_used = DEC_BATCH * n_pages
    n_pool = n_used + max(1, n_used // 4)
    w_buf = min(WINDOW, PAST_LEN)
    pool_shape = (N_NSA_LAYERS, n_pool, PAGE_SIZE, N_KV_HEADS, HEAD_DIM)
    win_shape = (N_NSA_LAYERS, DEC_BATCH, w_buf, N_KV_HEADS, HEAD_DIM)
    NC, NN = N_CONV_LAYERS, N_NSA_LAYERS
    return {
        'x_prompt': nrm((BATCH, SEQ, D_MODEL), 1.0),
        'x_sample': nrm((DEC_BATCH, DEC_SEQ, D_MODEL), 1.0),
        'state_conv': nrm((NC, DEC_BATCH, CONV_WIDTH - 1, D_MODEL), 0.5),
        'cache_k_cmp': nrm(pool_shape, 1.0),
        'cache_v_cmp': nrm(pool_shape, 1.0),
        'cache_k_sel': nrm(pool_shape, 1.0),
        'cache_v_sel': nrm(pool_shape, 1.0),
        'state_k_win': nrm(win_shape, 1.0),
        'state_v_win': nrm(win_shape, 1.0),
        'page_table': jax.random.permutation(next(ks), n_pool)[:n_used].reshape(DEC_BATCH, n_pages).astype(jnp.int32),
        'ln_g': 1.0 + nrm((DEPTH, 2, D_MODEL), 0.02),
        'ln_b': nrm((DEPTH, 2, D_MODEL), 0.02),
        'conv_w_pw1': nrm((NC, D_MODEL, 2 * D_MODEL), D_MODEL ** -0.5),
        'conv_b_pw1': nrm((NC, 2 * D_MODEL), 0.02),
        'conv_w_dw': nrm((NC, CONV_WIDTH, D_MODEL), CONV_WIDTH ** -0.5),
        'conv_b_dw': nrm((NC, D_MODEL), 0.02),
        'conv_ln_g': 1.0 + nrm((NC, D_MODEL), 0.02),
        'conv_ln_b': nrm((NC, D_MODEL), 0.02),
        'conv_w_pw2': nrm((NC, D_MODEL, D_MODEL), BETA * D_MODEL ** -0.5),
        'conv_b_pw2': nrm((NC, D_MODEL), 0.02),
        'nsa_w_in': nrm((NN, D_MODEL, IN_COLS), D_MODEL ** -0.5),
        'nsa_cmp_pe': nrm((NN, 2, CMP_LEN, HEAD_DIM), 0.1),
        'nsa_cmp_w1': nrm((NN, 2, CMP_LEN, HEAD_DIM, HEAD_DIM), (CMP_LEN * HEAD_DIM) ** -0.5),
        'nsa_cmp_w2': nrm((NN, 2, HEAD_DIM, HEAD_DIM), HEAD_DIM ** -0.5),
        'nsa_w_o': nrm((NN, Q_COLS, D_MODEL), BETA * Q_COLS ** -0.5),
        'ffn_w_gate': nrm((NC, D_MODEL, D_FF), D_MODEL ** -0.5),
        'ffn_w_up': nrm((NC, D_MODEL, D_FF), D_MODEL ** -0.5),
        'ffn_w_down': nrm((NC, D_FF, D_MODEL), BETA * D_FF ** -0.5),
        'moe_w_router': nrm((NN, D_MODEL, N_EXPERTS), D_MODEL ** -0.5),
        'moe_w_gate': nrm((NN, N_EXPERTS, D_MODEL, D_FF), D_MODEL ** -0.5),
        'moe_w_up': nrm((NN, N_EXPERTS, D_MODEL, D_FF), D_MODEL ** -0.5),
        'moe_w_down': nrm((NN, N_EXPERTS, D_FF, D_MODEL), BETA * D_FF ** -0.5),
    }


def reference(x_prompt, x_sample, state_conv, cache_k_cmp, cache_v_cmp, cache_k_sel, cache_v_sel,
              state_k_win, state_v_win, page_table, ln_g, ln_b,
              conv_w_pw1, conv_b_pw1, conv_w_dw, conv_b_dw, conv_ln_g, conv_ln_b, conv_w_pw2, conv_b_pw2,
              nsa_w_in, nsa_cmp_pe, nsa_cmp_w1, nsa_cmp_w2, nsa_w_o,
              ffn_w_gate, ffn_w_up, ffn_w_down,
              moe_w_router, moe_w_gate, moe_w_up, moe_w_down):
    yp, ys = x_prompt, x_sample
    conv_p, conv_s, nsa_p, nsa_s = [], [], [], []
    for i in range(DEPTH):
        j = i // N_MIXERS
        if i % N_MIXERS == 0:
            cw = (conv_w_pw1[j], conv_b_pw1[j], conv_w_dw[j], conv_b_dw[j],
                  conv_ln_g[j], conv_ln_b[j], conv_w_pw2[j], conv_b_pw2[j])
            zeros = jnp.zeros((yp.shape[0], CONV_WIDTH - 1, D_MODEL), yp.dtype)
            hp, st_p = _conv_mixer(yp, zeros, *cw)
            hs, st_s = _conv_mixer(ys, state_conv[j], *cw)
            conv_p.append(st_p)
            conv_s.append(st_s)
        else:
            nw = (nsa_w_in[j], nsa_cmp_pe[j], nsa_cmp_w1[j], nsa_cmp_w2[j], nsa_w_o[j])
            hp, st_p = _nsa_prompt(yp, *nw)
            hs, st_s = _nsa_sample(ys, cache_k_cmp[j], cache_v_cmp[j], cache_k_sel[j], cache_v_sel[j],
                                   state_k_win[j], state_v_win[j], page_table, *nw)
            nsa_p.append(st_p)
            nsa_s.append(st_s)
        yp = _layer_norm(ALPHA * yp + hp, ln_g[i, 0], ln_b[i, 0])
        ys = _layer_norm(ALPHA * ys + hs, ln_g[i, 0], ln_b[i, 0])
        f = i // 2
        if i % 2 == 0:
            fw = (ffn_w_gate[f], ffn_w_up[f], ffn_w_down[f])
            fp, fs = _swiglu(yp, *fw), _swiglu(ys, *fw)
        else:
            mw = (moe_w_router[f], moe_w_gate[f], moe_w_up[f], moe_w_down[f])
            fp, fs = _moe(yp, *mw), _moe(ys, *mw)
        yp = _layer_norm(ALPHA * yp + fp, ln_g[i, 1], ln_b[i, 1])
        ys = _layer_norm(ALPHA * ys + fs, ln_g[i, 1], ln_b[i, 1])
    new_state_conv_prompt = jnp.stack(conv_p)
    new_state_conv_sample = jnp.stack(conv_s)
    (new_cache_k_cmp_prompt, new_cache_v_cmp_prompt, new_cache_k_sel_prompt, new_cache_v_sel_prompt,
     new_state_k_win_prompt, new_state_v_win_prompt) = [jnp.stack(a) for a in zip(*nsa_p)]
    (new_cache_k_cmp_sample, new_cache_v_cmp_sample, new_cache_k_sel_sample, new_cache_v_sel_sample,
     new_state_k_win_sample, new_state_v_win_sample) = [jnp.stack(a) for a in zip(*nsa_s)]
    return (yp, ys,
            new_state_conv_prompt, new_cache_k_cmp_prompt, new_cache_v_cmp_prompt,
            new_cache_k_sel_prompt, new_cache_v_sel_prompt, new_state_k_win_prompt, new_state_v_win_prompt,
            new_state_conv_sample, new_cache_k_cmp_sample, new_cache_v_cmp_sample,
            new_cache_k_sel_sample, new_cache_v_sel_sample, new_state_k_win_sample, new_state_v_win_sample)
```

```python
import functools

import numpy as np
import jax
import jax.numpy as jnp
from jax import lax
from jax.experimental import pallas as pl
from jax.experimental.pallas import tpu as pltpu

F32 = jnp.float32
BF16 = jnp.bfloat16

N_HEADS = 16
N_KV_HEADS = 4
GROUP = N_HEADS // N_KV_HEADS
HEAD_DIM = 64
ROT_DIM = HEAD_DIM // 4
ROPE_THETA = 500000.0
CMP_LEN = 32
CMP_STRIDE = 16
SEL_BLOCK = 64
N_SELECT = 16
N_LOCAL_SEL = 2
WINDOW = 512
TOP_K = 2
LN_EPS = 1e-5
NEG = -1e30
BIG = 1e30

Q_TILE = 128
KEY_TILE = 256
KV_COLS = N_KV_HEADS * HEAD_DIM
Q_COLS = N_HEADS * HEAD_DIM
LANES = 128
VMEM_LIMIT = 56 * 1024 * 1024


def _cparams(*sem):
    return pltpu.CompilerParams(dimension_semantics=sem, vmem_limit_bytes=VMEM_LIMIT)


def _round_up(a, m):
    return -(-a // m) * m


def _row_tile(n, pref):
    t = min(n, pref)
    while n % t:
        t //= 2
    assert t >= 8 and n % t == 0
    return t


def _sigmoid(v):
    return 1.0 / (1.0 + jnp.exp(-v))


def _silu(v):
    return v * _sigmoid(v)


def _layer_norm(v, g, b):
    mu = jnp.mean(v, axis=-1, keepdims=True)
    d = v - mu
    var = jnp.mean(d * d, axis=-1, keepdims=True)
    return d * lax.rsqrt(var + LN_EPS) * g + b


def _dot(a, b):
    return jnp.dot(a, b, preferred_element_type=F32)


def _split_bf16(v):
    hi = v.astype(BF16)
    lo = (v - hi.astype(F32)).astype(BF16)
    return hi, lo


def _pw1_glu_kernel(x_ref, wa_ref, wg_ref, ba_ref, bg_ref, u_ref):
    x = x_ref[...].astype(BF16)
    a = _dot(x, wa_ref[...]) + ba_ref[...]
    g = _dot(x, wg_ref[...]) + bg_ref[...]
    u_ref[...] = a * _sigmoid(g)


def _pw1_glu(x, wa, wg, ba, bg):
    n, d = x.shape
    tm = _row_tile(n, 512)
    full = lambda i: (0, 0)
    return pl.pallas_call(
        _pw1_glu_kernel,
        grid=(n // tm,),
        in_specs=[pl.BlockSpec((tm, d), lambda i: (i, 0)),
                  pl.BlockSpec(wa.shape, full), pl.BlockSpec(wg.shape, full),
                  pl.BlockSpec(ba.shape, full), pl.BlockSpec(bg.shape, full)],
        out_specs=pl.BlockSpec((tm, d), lambda i: (i, 0)),
        out_shape=jax.ShapeDtypeStruct((n, d), F32),
        compiler_params=_cparams("parallel"),
    )(x, wa, wg, ba, bg)


HALO = 32


def _dw_taps(ext_ref, wdw_ref, rows, width):
    off = HALO - (width - 1)
    acc = wdw_ref[0:1, :] * ext_ref[pl.ds(off, rows), :]
    for k in range(1, width):
        acc = acc + wdw_ref[k:k + 1, :] * ext_ref[pl.ds(off + k, rows), :]
    return acc


def _conv2_prompt_kernel(width, alpha, u_ref, halo_ref, x_ref, wdw_ref, bdw_ref, cg_ref, cb_ref,
                         w2_ref, b2_ref, g_ref, b_ref, o_ref, ext_ref):
    i = pl.program_id(1)
    tm = u_ref.shape[1]

    @pl.when(i == 0)
    def _():
        ext_ref[0:HALO, :] = jnp.zeros((HALO, ext_ref.shape[1]), F32)

    @pl.when(i > 0)
    def _():
        ext_ref[0:HALO, :] = halo_ref[0]

    ext_ref[HALO:, :] = u_ref[0]
    c = _dw_taps(ext_ref, wdw_ref, tm, width) + bdw_ref[...]
    c = _layer_norm(c, cg_ref[...], cb_ref[...])
    h = _dot(_silu(c).astype(BF16), w2_ref[...]) + b2_ref[...]
    o_ref[0] = _layer_norm(alpha * x_ref[0] + h, g_ref[...], b_ref[...])


def _conv2_prompt(u, x, wdw, bdw, cg, cb, w2, b2, g, b, alpha):
    bsz, t, d = u.shape
    width = wdw.shape[0]
    tm = _row_tile(t, 256)
    assert tm % HALO == 0 and width - 1 <= HALO
    r = tm // HALO
    full = lambda bi, i: (0, 0)
    vec = pl.BlockSpec((1, d), full)
    return pl.pallas_call(
        functools.partial(_conv2_prompt_kernel, width, alpha),
        grid=(bsz, t // tm),
        in_specs=[pl.BlockSpec((1, tm, d), lambda bi, i: (bi, i, 0)),
                  pl.BlockSpec((1, HALO, d), lambda bi, i: (bi, jnp.maximum(i * r - 1, 0), 0)),
                  pl.BlockSpec((1, tm, d), lambda bi, i: (bi, i, 0)),
                  pl.BlockSpec(wdw.shape, full), vec, vec, vec,
                  pl.BlockSpec(w2.shape, full), vec, vec, vec],
        out_specs=pl.BlockSpec((1, tm, d), lambda bi, i: (bi, i, 0)),
        out_shape=jax.ShapeDtypeStruct((bsz, t, d), F32),
        scratch_shapes=[pltpu.VMEM((HALO + tm, d), F32)],
        compiler_params=_cparams("parallel", "arbitrary"),
    )(u, u, x, wdw, bdw, cg, cb, w2, b2, g, b)


def _conv2_sample_kernel(width, alpha, ext_ref, x_ref, wdw_ref, bdw_ref, cg_ref, cb_ref,
                         w2_ref, b2_ref, g_ref, b_ref, o_ref):
    bsz, rows, d = x_ref.shape
    acc = wdw_ref[0:1, :][None] * ext_ref[:, pl.ds(0, rows), :]
    for k in range(1, width):
        acc = acc + wdw_ref[k:k + 1, :][None] * ext_ref[:, pl.ds(k, rows), :]
    c = acc.reshape(bsz * rows, d) + bdw_ref[...]
    c = _layer_norm(c, cg_ref[...], cb_ref[...])
    h = _dot(_silu(c).astype(BF16), w2_ref[...]) + b2_ref[...]
    x = x_ref[...].reshape(bsz * rows, d)
    o_ref[...] = _layer_norm(alpha * x + h, g_ref[...], b_ref[...]).reshape(bsz, rows, d)


def _conv2_sample(ext, x, wdw, bdw, cg, cb, w2, b2, g, b, alpha):
    bsz, rows, d = x.shape
    return pl.pallas_call(
        functools.partial(_conv2_sample_kernel, wdw.shape[0], alpha),
        out_shape=jax.ShapeDtypeStruct((bsz, rows, d), F32),
        compiler_params=pltpu.CompilerParams(vmem_limit_bytes=VMEM_LIMIT),
    )(ext, x, wdw, bdw, cg, cb, w2, b2, g, b)


def _ffn_kernel(alpha, x_ref, wg_ref, wu_ref, wd_ref, g_ref, b_ref, o_ref, acc_ref):
    k = pl.program_id(1)

    @pl.when(k == 0)
    def _():
        acc_ref[...] = jnp.zeros(acc_ref.shape, F32)

    x = x_ref[...].astype(BF16)
    h = _silu(_dot(x, wg_ref[...])) * _dot(x, wu_ref[...])
    acc_ref[...] += _dot(h.astype(BF16), wd_ref[...])

    @pl.when(k == pl.num_programs(1) - 1)
    def _():
        o_ref[...] = _layer_norm(alpha * x_ref[...] + acc_ref[...], g_ref[...], b_ref[...])


def _ff_tile(dff):
    for tf in (512, 384, 256, 128):
        if dff % tf == 0:
            return tf
    return dff


def _ffn_ln(x, wg, wu, wd, g, b, alpha):
    n, d = x.shape
    dff = wg.shape[1]
    tm = _row_tile(n, 1024)
    tf = _ff_tile(dff)
    vec = pl.BlockSpec((1, d), lambda i, k: (0, 0))
    return pl.pallas_call(
        functools.partial(_ffn_kernel, alpha),
        grid=(n // tm, dff // tf),
        in_specs=[pl.BlockSpec((tm, d), lambda i, k: (i, 0)),
                  pl.BlockSpec((d, tf), lambda i, k: (0, k)),
                  pl.BlockSpec((d, tf), lambda i, k: (0, k)),
                  pl.BlockSpec((tf, d), lambda i, k: (k, 0)), vec, vec],
        out_specs=pl.BlockSpec((tm, d), lambda i, k: (i, 0)),
        out_shape=jax.ShapeDtypeStruct((n, d), F32),
        scratch_shapes=[pltpu.VMEM((tm, d), F32)],
        compiler_params=_cparams("parallel", "arbitrary"),
    )(x, wg, wu, wd, g, b)


def _router_kernel(n_exp, x_ref, wh_ref, wl_ref, comb_ref):
    xh, xl = _split_bf16(x_ref[...])
    logits = _dot(xh, wh_ref[...]) + (_dot(xh, wl_ref[...]) + _dot(xl, wh_ref[...]))
    lane = lax.broadcasted_iota(jnp.int32, logits.shape, 1)
    logits = jnp.where(lane < n_exp, logits, -jnp.inf)
    m1 = jnp.max(logits, axis=-1, keepdims=True)
    i1 = jnp.min(jnp.where(logits == m1, lane, LANES), axis=-1, keepdims=True)
    rest = jnp.where(lane == i1, -jnp.inf, logits)
    m2 = jnp.max(rest, axis=-1, keepdims=True)
    i2 = jnp.min(jnp.where(rest == m2, lane, LANES), axis=-1, keepdims=True)
    e2 = jnp.exp(m2 - m1)
    w1 = 1.0 / (1.0 + e2)
    w2 = e2 / (1.0 + e2)
    comb_ref[...] = jnp.where(lane == i1, w1, 0.0) + jnp.where(lane == i2, w2, 0.0)


def _router(x, w_router):
    n, d = x.shape
    n_exp = w_router.shape[1]
    assert TOP_K == 2 and n_exp <= LANES
    wpad = jnp.pad(w_router, ((0, 0), (0, LANES - n_exp)))
    wh = wpad.astype(BF16)
    wl = (wpad - wh.astype(F32)).astype(BF16)
    tm = _row_tile(n, 512)
    return pl.pallas_call(
        functools.partial(_router_kernel, n_exp),
        grid=(n // tm,),
        in_specs=[pl.BlockSpec((tm, d), lambda i: (i, 0)),
                  pl.BlockSpec((d, LANES), lambda i: (0, 0)),
                  pl.BlockSpec((d, LANES), lambda i: (0, 0))],
        out_specs=pl.BlockSpec((tm, LANES), lambda i: (i, 0)),
        out_shape=jax.ShapeDtypeStruct((n, LANES), F32),
        compiler_params=_cparams("parallel"),
    )(x, wh, wl)


def _moe_kernel(alpha, x_ref, comb_ref, wg_ref, wu_ref, wd_ref, g_ref, b_ref, o_ref, acc_ref):
    e = pl.program_id(1)
    k = pl.program_id(2)

    @pl.when((e == 0) & (k == 0))
    def _():
        acc_ref[...] = jnp.zeros(acc_ref.shape, F32)

    comb = comb_ref[...]
    lane = lax.broadcasted_iota(jnp.int32, comb.shape, 1)
    c = jnp.sum(jnp.where(lane == e, comb, 0.0), axis=-1, keepdims=True)
    x = x_ref[...].astype(BF16)
    h = _silu(_dot(x, wg_ref[0])) * _dot(x, wu_ref[0])
    acc_ref[...] += c * _dot(h.astype(BF16), wd_ref[0])

    @pl.when((e == pl.num_programs(1) - 1) & (k == pl.num_programs(2) - 1))
    def _():
        o_ref[...] = _layer_norm(alpha * x_ref[...] + acc_ref[...], g_ref[...], b_ref[...])


def _moe_ln(x, comb, wg, wu, wd, g, b, alpha):
    n, d = x.shape
    n_exp, _, dff = wg.shape
    tm = _row_tile(n, 1024)
    tf = _ff_tile(dff)
    vec = pl.BlockSpec((1, d), lambda i, e, k: (0, 0))
    return pl.pallas_call(
        functools.partial(_moe_kernel, alpha),
        grid=(n // tm, n_exp, dff // tf),
        in_specs=[pl.BlockSpec((tm, d), lambda i, e, k: (i, 0)),
                  pl.BlockSpec((tm, LANES), lambda i, e, k: (i, 0)),
                  pl.BlockSpec((1, d, tf), lambda i, e, k: (e, 0, k)),
                  pl.BlockSpec((1, d, tf), lambda i, e, k: (e, 0, k)),
                  pl.BlockSpec((1, tf, d), lambda i, e, k: (e, k, 0)), vec, vec],
        out_specs=pl.BlockSpec((tm, d), lambda i, e, k: (i, 0)),
        out_shape=jax.ShapeDtypeStruct((n, d), F32),
        scratch_shapes=[pltpu.VMEM((tm, d), F32)],
        compiler_params=_cparams("parallel", "arbitrary", "arbitrary"),
    )(x, comb, wg, wu, wd, g, b)


def _rope_chunk(v, cos, sa, sb):
    half = ROT_DIM // 2
    return v * cos + pltpu.roll(v, LANES - half, 1) * sa + pltpu.roll(v, half, 1) * sb


def _nsa_proj_kernel(x_ref, w_ref, cos_ref, sa_ref, sb_ref,
                     q_ref, kc_ref, vc_ref, ks_ref, vs_ref, kw_ref, vw_ref, gate_ref):
    y = _dot(x_ref[...].astype(BF16), w_ref[...])
    cos, sa, sb = cos_ref[...], sa_ref[...], sb_ref[...]
    scale = HEAD_DIM ** -0.5
    for j in range(Q_COLS // LANES):
        q_ref[:, j * LANES:(j + 1) * LANES] = (
            _rope_chunk(y[:, j * LANES:(j + 1) * LANES], cos, sa, sb) * scale).astype(BF16)
    base = Q_COLS
    for idx, (ref, roped) in enumerate(((kc_ref, False), (vc_ref, False), (ks_ref, True),
                                        (vs_ref, False), (kw_ref, True), (vw_ref, False))):
        for j in range(KV_COLS // LANES):
            lo = base + idx * KV_COLS + j * LANES
            v = y[:, lo:lo + LANES]
            ref[:, j * LANES:(j + 1) * LANES] = _rope_chunk(v, cos, sa, sb) if roped else v
    gate_ref[...] = _sigmoid(y[:, base + 6 * KV_COLS:base + 6 * KV_COLS + LANES])


def _rope_tables(pos):
    half = ROT_DIM // 2
    inv_freq = ROPE_THETA ** (-jnp.arange(half, dtype=F32) / half)
    ang = pos.astype(F32)[:, None] * inv_freq[None, :]
    cos, sin = jnp.cos(ang), jnp.sin(ang)
    n = pos.shape[0]
    rest = HEAD_DIM - ROT_DIM
    one = jnp.ones((n, rest), F32)
    zero = jnp.zeros((n, rest), F32)
    zh = jnp.zeros((n, half), F32)
    c = jnp.concatenate([cos, cos, one], axis=1)
    sa = jnp.concatenate([-sin, zh, zero], axis=1)
    sb = jnp.concatenate([zh, sin, zero], axis=1)
    rep = LANES // HEAD_DIM
    return tuple(jnp.tile(t, (1, rep)) for t in (c, sa, sb))


def _nsa_proj(x, w_pad, tables, n_tab_blocks):
    n, d = x.shape
    tm = tables[0].shape[0] // n_tab_blocks
    assert n % tm == 0
    row = lambda i: (i, 0)
    tab = pl.BlockSpec((tm, LANES), lambda i: (i % n_tab_blocks, 0))
    kv_spec = pl.BlockSpec((tm, KV_COLS), row)
    kv_shape = jax.ShapeDtypeStruct((n, KV_COLS), F32)
    return pl.pallas_call(
        _nsa_proj_kernel,
        grid=(n // tm,),
        in_specs=[pl.BlockSpec((tm, d), row), pl.BlockSpec(w_pad.shape, lambda i: (0, 0)), tab, tab, tab],
        out_specs=[pl.BlockSpec((tm, Q_COLS), row)] + [kv_spec] * 6 + [pl.BlockSpec((tm, LANES), row)],
        out_shape=[jax.ShapeDtypeStruct((n, Q_COLS), BF16)] + [kv_shape] * 6
                  + [jax.ShapeDtypeStruct((n, LANES), F32)],
        compiler_params=_cparams("parallel"),
    )(x, w_pad, *tables)


def _compress_kernel(roped, g_ref, pea_ref, peb_ref, wa_ref, wb_ref, w2_ref, cos_ref, sa_ref, sb_ref, o_ref):
    grp = g_ref[0]
    rows = grp.shape[0]
    a = _dot((grp + pea_ref[...]).astype(BF16), wa_ref[...])
    bm = _dot((grp + peb_ref[...]).astype(BF16), wb_ref[...])
    h = a + pltpu.roll(bm, rows - 1, 0)
    out = _dot(_silu(h).astype(BF16), w2_ref[...])
    if roped:
        cos, sa, sb = cos_ref[...], sa_ref[...], sb_ref[...]
        for j in range(KV_COLS // LANES):
            o_ref[0, :, j * LANES:(j + 1) * LANES] = _rope_chunk(out[:, j * LANES:(j + 1) * LANES], cos, sa, sb)
    else:
        o_ref[0] = out


def _compress(groups, pe, w1, w2, roped):
    bsz, ng, gw = groups.shape
    assert CMP_LEN == 2 * CMP_STRIDE and gw == CMP_STRIDE * KV_COLS
    eye = jnp.eye(N_KV_HEADS, dtype=F32)
    blockdiag = lambda w: jnp.einsum('lde,hg->lhdge', w, eye).reshape(-1, KV_COLS).astype(BF16)
    wa, wb = blockdiag(w1[:CMP_STRIDE]), blockdiag(w1[CMP_STRIDE:])
    w2b = jnp.einsum('de,hg->hdge', w2, eye).reshape(KV_COLS, KV_COLS).astype(BF16)
    pe_row = lambda p: jnp.tile(p[:, None, :], (1, N_KV_HEADS, 1)).reshape(1, gw)
    pea, peb = pe_row(pe[:CMP_STRIDE]), pe_row(pe[CMP_STRIDE:])
    c_end = jnp.arange(ng) * CMP_STRIDE + (CMP_LEN - 1)
    tables = _rope_tables(c_end)
    full = lambda b: (0, 0)
    tab = pl.BlockSpec((ng, LANES), full)
    return pl.pallas_call(
        functools.partial(_compress_kernel, roped),
        grid=(bsz,),
        in_specs=[pl.BlockSpec((1, ng, gw), lambda b: (b, 0, 0)),
                  pl.BlockSpec((1, gw), full), pl.BlockSpec((1, gw), full),
                  pl.BlockSpec(wa.shape, full), pl.BlockSpec(wb.shape, full), pl.BlockSpec(w2b.shape, full),
                  tab, tab, tab],
        out_specs=pl.BlockSpec((1, ng, KV_COLS), lambda b: (b, 0, 0)),
        out_shape=jax.ShapeDtypeStruct((bsz, ng, KV_COLS), F32),
        compiler_params=_cparams("parallel"),
    )(groups, pea, peb, wa, wb, w2b, *tables)


def _softmax_cols(s, mask):
    s = jnp.where(mask, s, NEG)
    m = jnp.max(s, axis=0, keepdims=True)
    p = jnp.where(mask, jnp.exp(s - m), 0.0)
    return p * (1.0 / jnp.maximum(jnp.sum(p, axis=0, keepdims=True), 1e-30))


def _attn_kernel(qt0, n_cmp, n_blk, w_base, qT_ref, kc_ref, vcT_ref, kaug_ref, vsT_ref, kw_ref, vwT_ref,
                 gate_ref, ovT_ref, o_ref):
    qt = pl.program_id(2) + qt0
    q0 = qt * Q_TILE
    qT = qT_ref[0, 0, 0]
    cols = qT.shape[1]
    qpos = q0 + lax.broadcasted_iota(jnp.int32, (1, cols), 1) % Q_TILE
    nb_pad = ovT_ref.shape[0]

    kc = kc_ref[0, 0]
    ncp = kc.shape[0]
    cidx = lax.broadcasted_iota(jnp.int32, (ncp, 1), 0)
    c_end = cidx * CMP_STRIDE + (CMP_LEN - 1)
    p_c = _softmax_cols(_dot(kc, qT), (c_end <= qpos) & (cidx < n_cmp))
    o_c = _dot(vcT_ref[0, 0], p_c.astype(BF16))

    p_sum = p_c[:, 0:Q_TILE]
    for g in range(1, GROUP):
        p_sum = p_sum + p_c[:, g * Q_TILE:(g + 1) * Q_TILE]
    p_hi, p_lo = _split_bf16(p_sum)
    imp = _dot(ovT_ref[...], p_hi) + _dot(ovT_ref[...], p_lo)

    blk = lax.broadcasted_iota(jnp.int32, (nb_pad, 1), 0)
    qp = qpos[:, 0:Q_TILE]
    lag = qp // SEL_BLOCK - blk
    forced = (blk == 0) | ((lag >= 0) & (lag < N_LOCAL_SEL))
    score = jnp.where(forced, BIG, jnp.where(blk * SEL_BLOCK <= qp, imp, -BIG))
    score = jnp.where(blk < n_blk, score, -jnp.inf)
    bias = jnp.full(score.shape, NEG, F32)
    for _ in range(min(N_SELECT, n_blk)):
        m = jnp.max(score, axis=0, keepdims=True)
        j = jnp.min(jnp.where(score == m, blk, nb_pad), axis=0, keepdims=True)
        hit = blk == j
        bias = jnp.where(hit, 0.0, bias)
        score = jnp.where(hit, -jnp.inf, score)
    bias = bias.astype(BF16)
    q_aug = jnp.concatenate([jnp.concatenate([bias] * GROUP, axis=1), qT], axis=0)

    def body(kb, carry):
        m, l, acc = carry
        k0 = pl.multiple_of(kb * KEY_TILE, KEY_TILE)
        s = _dot(kaug_ref[0, 0, pl.ds(k0, KEY_TILE), :], q_aug)
        kpos = k0 + lax.broadcasted_iota(jnp.int32, (KEY_TILE, 1), 0)
        s = jnp.where(kpos <= qpos, s, NEG)
        m_new = jnp.maximum(m, jnp.max(s, axis=0, keepdims=True))
        alpha = jnp.exp(m - m_new)
        p = jnp.exp(s - m_new)
        l = alpha * l + jnp.sum(p, axis=0, keepdims=True)
        acc = alpha * acc + _dot(vsT_ref[0, 0, :, pl.ds(k0, KEY_TILE)], p.astype(BF16))
        return m_new, l, acc

    n_kb = (q0 + Q_TILE + KEY_TILE - 1) // KEY_TILE
    init = (jnp.full((1, cols), NEG, F32), jnp.zeros((1, cols), F32), jnp.zeros((HEAD_DIM, cols), F32))
    _, l_s, acc_s = lax.fori_loop(0, n_kb, body, init)
    o_s = acc_s / jnp.maximum(l_s, 1e-30)

    wlen = WINDOW + Q_TILE
    w0 = pl.multiple_of(jnp.maximum(q0 - WINDOW, w_base) - w_base, Q_TILE)
    wpos = w0 + w_base + lax.broadcasted_iota(jnp.int32, (wlen, 1), 0)
    p_w = _softmax_cols(_dot(kw_ref[0, 0, pl.ds(w0, wlen), :], qT),
                        (wpos <= qpos) & (wpos > qpos - WINDOW))
    o_w = _dot(vwT_ref[0, 0, :, pl.ds(w0, wlen)], p_w.astype(BF16))

    gate = gate_ref[0, 0, 0]
    o_ref[0, 0, 0] = (gate[0:1] * o_c + gate[1:2] * o_s + gate[2:3] * o_w).astype(BF16)


def _overlap_t(n_cmp, n_cmp_pad, n_blk, nb_pad):
    cs = np.arange(n_cmp_pad)[None, :] * CMP_STRIDE
    ss = np.arange(nb_pad)[:, None] * SEL_BLOCK
    ov = np.maximum(np.minimum(cs + CMP_LEN, ss + SEL_BLOCK) - np.maximum(cs, ss), 0) / CMP_LEN
    ov = ov * (np.arange(n_cmp_pad)[None, :] < n_cmp) * (np.arange(nb_pad)[:, None] < n_blk)
    return jnp.asarray(ov, BF16)


def _nsa_attention(q, kc, vc, ks, vs, kw, vw, gates, *, t_total, qt0, n_qt, w_base):
    bsz = q.shape[0]
    tk = ks.shape[1]
    n_cmp = (t_total - CMP_LEN) // CMP_STRIDE + 1
    n_blk = -(-t_total // SEL_BLOCK)
    nb_pad = _round_up(tk // SEL_BLOCK, 16)
    ncp = kc.shape[1]
    cols = GROUP * Q_TILE
    assert tk % KEY_TILE == 0 and tk >= _round_up((qt0 + n_qt) * Q_TILE, KEY_TILE) and n_cmp <= ncp

    heads = lambda a: a.astype(BF16).reshape(bsz, a.shape[1], N_KV_HEADS, HEAD_DIM).transpose(0, 2, 1, 3)
    heads_t = lambda a: a.astype(BF16).reshape(bsz, a.shape[1], N_KV_HEADS, HEAD_DIM).transpose(0, 2, 3, 1)
    qT = q.reshape(bsz, n_qt, Q_TILE, N_KV_HEADS, GROUP, HEAD_DIM).transpose(0, 3, 1, 5, 4, 2)
    qT = qT.reshape(bsz, N_KV_HEADS, n_qt, HEAD_DIM, cols)
    onehot = jnp.asarray(np.arange(tk)[:, None] // SEL_BLOCK == np.arange(nb_pad)[None, :], BF16)
    kaug = jnp.concatenate([jnp.broadcast_to(onehot, (bsz, N_KV_HEADS, tk, nb_pad)), heads(ks)], axis=-1)
    gT = gates[..., :3 * N_HEADS].reshape(bsz, n_qt, Q_TILE, N_KV_HEADS, GROUP, 3).transpose(0, 3, 1, 5, 4, 2)
    gT = jnp.pad(gT.reshape(bsz, N_KV_HEADS, n_qt, 3, cols), ((0, 0), (0, 0), (0, 0), (0, 5), (0, 0)))
    ovT = _overlap_t(n_cmp, ncp, n_blk, nb_pad)
    lw = kw.shape[1]

    per_head = lambda *shape: pl.BlockSpec((1, 1) + shape, lambda b, h, i: (b, h, 0, 0))
    per_tile = lambda *shape: pl.BlockSpec((1, 1, 1) + shape, lambda b, h, i: (b, h, i, 0, 0))
    oT = pl.pallas_call(
        functools.partial(_attn_kernel, qt0, n_cmp, n_blk, w_base),
        grid=(bsz, N_KV_HEADS, n_qt),
        in_specs=[per_tile(HEAD_DIM, cols),
                  per_head(ncp, HEAD_DIM), per_head(HEAD_DIM, ncp),
                  per_head(tk, nb_pad + HEAD_DIM), per_head(HEAD_DIM, tk),
                  per_head(lw, HEAD_DIM), per_head(HEAD_DIM, lw),
                  per_tile(8, cols),
                  pl.BlockSpec(ovT.shape, lambda b, h, i: (0, 0))],
        out_specs=per_tile(HEAD_DIM, cols),
        out_shape=jax.ShapeDtypeStruct((bsz, N_KV_HEADS, n_qt, HEAD_DIM, cols), BF16),
        compiler_params=_cparams("parallel", "parallel", "arbitrary"),
    )(qT, heads(kc), heads_t(vc), kaug, heads_t(vs), heads(kw), heads_t(vw), gT, ovT)
    o = oT.reshape(bsz, N_KV_HEADS, n_qt, HEAD_DIM, GROUP, Q_TILE).transpose(0, 2, 5, 1, 4, 3)
    return o.reshape(bsz, n_qt * Q_TILE, Q_COLS)


def _proj_ln_kernel(alpha, a_ref, w_ref, x_ref, g_ref, b_ref, o_ref):
    h = _dot(a_ref[...], w_ref[...])
    o_ref[...] = _layer_norm(alpha * x_ref[...] + h, g_ref[...], b_ref[...])


def _proj_ln(a, w, x, g, b, alpha):
    n, d = x.shape
    tm = _row_tile(n, 512)
    vec = pl.BlockSpec((1, d), lambda i: (0, 0))
    return pl.pallas_call(
        functools.partial(_proj_ln_kernel, alpha),
        grid=(n // tm,),
        in_specs=[pl.BlockSpec((tm, a.shape[1]), lambda i: (i, 0)), pl.BlockSpec(w.shape, lambda i: (0, 0)),
                  pl.BlockSpec((tm, d), lambda i: (i, 0)), vec, vec],
        out_specs=pl.BlockSpec((tm, d), lambda i: (i, 0)),
        out_shape=jax.ShapeDtypeStruct((n, d), F32),
        compiler_params=_cparams("parallel"),
    )(a, w, x, g, b)


def _conv_layer(yp, ys, state, w_pw1, b_pw1, w_dw, b_dw, cg, cb, w_pw2, b_pw2, g, b, alpha):
    bsz, t, d = yp.shape
    bd, s, _ = ys.shape
    width = w_dw.shape[0]
    wa, wg = w_pw1[:, :d].astype(BF16), w_pw1[:, d:].astype(BF16)
    ba, bg = b_pw1[None, :d], b_pw1[None, d:]
    w2 = w_pw2.astype(BF16)
    row = lambda v: v[None, :]
    tail = (w_dw, row(b_dw), row(cg), row(cb), w2, row(b_pw2), row(g), row(b), alpha)

    u_p = _pw1_glu(yp.reshape(bsz * t, d), wa, wg, ba, bg).reshape(bsz, t, d)
    new_p = _conv2_prompt(u_p, yp, *tail)
    st_p = u_p[:, t - (width - 1):]

    u_s = _pw1_glu(ys.reshape(bd * s, d), wa, wg, ba, bg).reshape(bd, s, d)
    ext = jnp.concatenate([state, u_s], axis=1)
    rows = _round_up(s, 8)
    ext_pad = jnp.pad(ext, ((0, 0), (0, _round_up(width - 1 + rows, 8) - ext.shape[1]), (0, 0)))
    xs_pad = jnp.pad(ys, ((0, 0), (0, rows - s), (0, 0)))
    new_s = _conv2_sample(ext_pad, xs_pad, *tail)[:, :s]
    st_s = ext[:, s:]
    return new_p, new_s, st_p, st_s


def _nsa_layer(yp, ys, pool_kc, pool_vc, pool_ks, pool_vs, kw_buf, vw_buf, page_table,
               w_in, pe, w1, w2, w_o, g, b, alpha):
    bsz, t, d = yp.shape
    bd, s, _ = ys.shape
    page = pool_kc.shape[1]
    past = page_table.shape[1] * page
    w_buf = kw_buf.shape[1]
    in_cols = w_in.shape[1]
    w_pad = jnp.pad(w_in, ((0, 0), (0, Q_COLS + 6 * KV_COLS + LANES - in_cols))).astype(BF16)
    w_o16 = w_o.astype(BF16)
    g, b = g[None, :], b[None, :]
    kv4 = lambda a, nb, nt: a.reshape(nb, nt, N_KV_HEADS, HEAD_DIM)

    tm = _row_tile(t, 256)
    tabs = _rope_tables(jnp.arange(t))
    q, k_c, v_c, k_s, v_s, k_w, v_w, gates = _nsa_proj(yp.reshape(bsz * t, d), w_pad, tabs, t // tm)
    b3 = lambda a: a.reshape(bsz, t, -1)
    grp = lambda a: a.reshape(bsz, t // CMP_STRIDE, CMP_STRIDE * KV_COLS)
    kc = _compress(grp(k_c), pe[0], w1[0], w2[0], True)
    vc = _compress(grp(v_c), pe[1], w1[1], w2[1], False)
    tk = _round_up(t, KEY_TILE)
    padk = lambda a: jnp.pad(b3(a), ((0, 0), (0, tk - t), (0, 0)))
    o = _nsa_attention(b3(q), kc, vc, padk(k_s), padk(v_s), b3(k_w), b3(v_w), b3(gates),
                       t_total=t, qt0=0, n_qt=t // Q_TILE, w_base=0)
    new_p = _proj_ln(o.reshape(bsz * t, Q_COLS), w_o16, yp.reshape(bsz * t, d), g, b, alpha).reshape(bsz, t, d)
    win = lambda a: jnp.concatenate([jnp.zeros((bsz, w_buf, KV_COLS), F32), b3(a)], axis=1)[:, -w_buf:]
    st_p = tuple(kv4(a, bsz, t) for a in (k_c, v_c, k_s, v_s)) + (
        kv4(win(k_w), bsz, w_buf), kv4(win(v_w), bsz, w_buf))

    assert past % Q_TILE == 0 and s <= Q_TILE
    n_s = bd * s
    tabs_s = tuple(jnp.tile(tt, (bd, 1)) for tt in _rope_tables(past + jnp.arange(s)))
    qs, k_c2, v_c2, k_s2, v_s2, k_w2, v_w2, gates_s = _nsa_proj(ys.reshape(n_s, d), w_pad, tabs_s, 1)
    s3 = lambda a: a.reshape(bd, s, -1)
    gather = lambda pool: pool[page_table].reshape(bd, past, KV_COLS)
    full_kc, full_vc = gather(pool_kc), gather(pool_vc)
    assert (past + s - CMP_LEN) // CMP_STRIDE + 1 <= past // CMP_STRIDE
    grp_s = lambda a: a.reshape(bd, past // CMP_STRIDE, CMP_STRIDE * KV_COLS)
    kc_s = _compress(grp_s(full_kc), pe[0], w1[0], w2[0], True)
    vc_s = _compress(grp_s(full_vc), pe[1], w1[1], w2[1], False)
    tk_s = _round_up(past + Q_TILE, KEY_TILE)
    full = lambda pool, new: jnp.pad(jnp.concatenate([gather(pool), s3(new)], axis=1),
                                     ((0, 0), (0, tk_s - past - s), (0, 0)))
    kw_all = jnp.concatenate([kw_buf.reshape(bd, w_buf, KV_COLS), s3(k_w2)], axis=1)
    vw_all = jnp.concatenate([vw_buf.reshape(bd, w_buf, KV_COLS), s3(v_w2)], axis=1)
    w_base = past - w_buf
    lw = WINDOW + Q_TILE
    assert w_buf == WINDOW
    padw = lambda a: jnp.pad(a, ((0, 0), (0, lw - a.shape[1]), (0, 0)))
    padq = lambda a: jnp.pad(s3(a), ((0, 0), (0, Q_TILE - s), (0, 0)))
    o_s = _nsa_attention(padq(qs), kc_s, vc_s, full(pool_ks, k_s2), full(pool_vs, v_s2),
                         padw(kw_all), padw(vw_all), padq(gates_s),
                         t_total=past + s, qt0=past // Q_TILE, n_qt=1, w_base=w_base)[:, :s]
    new_s = _proj_ln(o_s.reshape(n_s, Q_COLS), w_o16, ys.reshape(n_s, d), g, b, alpha).reshape(bd, s, d)
    st_s = tuple(kv4(a, bd, s) for a in (k_c2, v_c2, k_s2, v_s2)) + (
        kv4(kw_all[:, -w_buf:], bd, w_buf), kv4(vw_all[:, -w_buf:], bd, w_buf))
    return new_p, new_s, st_p, st_s


def kernel(x_prompt, x_sample, state_conv, cache_k_cmp, cache_v_cmp, cache_k_sel, cache_v_sel, state_k_win, state_v_win, page_table, ln_g, ln_b, conv_w_pw1, conv_b_pw1, conv_w_dw, conv_b_dw, conv_ln_g, conv_ln_b, conv_w_pw2, conv_b_pw2, nsa_w_in, nsa_cmp_pe, nsa_cmp_w1, nsa_cmp_w2, nsa_w_o, ffn_w_gate, ffn_w_up, ffn_w_down, moe_w_router, moe_w_gate, moe_w_up, moe_w_down):
    depth = ln_g.shape[0]
    alpha = (2 * depth) ** 0.25
    bsz, t, d = x_prompt.shape
    bd, s, _ = x_sample.shape
    yp, ys = x_prompt, x_sample
    conv_p, conv_s, nsa_p, nsa_s = [], [], [], []
    for i in range(depth):
        j = i // 2
        if i % 2 == 0:
            yp, ys, st_p, st_s = _conv_layer(
                yp, ys, state_conv[j], conv_w_pw1[j], conv_b_pw1[j], conv_w_dw[j], conv_b_dw[j],
                conv_ln_g[j], conv_ln_b[j], conv_w_pw2[j], conv_b_pw2[j], ln_g[i, 0], ln_b[i, 0], alpha)
            conv_p.append(st_p)
            conv_s.append(st_s)
        else:
            yp, ys, st_p, st_s = _nsa_layer(
                yp, ys, cache_k_cmp[j], cache_v_cmp[j], cache_k_sel[j], cache_v_sel[j],
                state_k_win[j], state_v_win[j], page_table,
                nsa_w_in[j], nsa_cmp_pe[j], nsa_cmp_w1[j], nsa_cmp_w2[j], nsa_w_o[j],
                ln_g[i, 0], ln_b[i, 0], alpha)
            nsa_p.append(st_p)
            nsa_s.append(st_s)
        g2, b2 = ln_g[i, 1][None, :], ln_b[i, 1][None, :]
        xp, xs = yp.reshape(bsz * t, d), ys.reshape(bd * s, d)
        if i % 2 == 0:
            fw = (ffn_w_gate[j].astype(BF16), ffn_w_up[j].astype(BF16), ffn_w_down[j].astype(BF16))
            xp = _ffn_ln(xp, *fw, g2, b2, alpha)
            xs = _ffn_ln(xs, *fw, g2, b2, alpha)
        else:
            mw = (moe_w_gate[j].astype(BF16), moe_w_up[j].astype(BF16), moe_w_down[j].astype(BF16))
            xp = _moe_ln(xp, _router(xp, moe_w_router[j]), *mw, g2, b2, alpha)
            xs = _moe_ln(xs, _router(xs, moe_w_router[j]), *mw, g2, b2, alpha)
        yp, ys = xp.reshape(bsz, t, d), xs.reshape(bd, s, d)
    stack = lambda parts: tuple(jnp.stack(a) for a in zip(*parts))
    return ((yp, ys, jnp.stack(conv_p)) + stack(nsa_p) + (jnp.stack(conv_s),) + stack(nsa_s))
```

```python
import functools

import numpy as np
import jax
import jax.numpy as jnp
from jax import lax
from jax.experimental import pallas as pl
from jax.experimental.pallas import tpu as pltpu

F32 = jnp.float32
BF16 = jnp.bfloat16

N_HEADS = 16
N_KV_HEADS = 4
GROUP = N_HEADS // N_KV_HEADS
HEAD_DIM = 64
ROT_DIM = HEAD_DIM // 4
ROPE_THETA = 500000.0
CMP_LEN = 32
CMP_STRIDE = 16
SEL_BLOCK = 64
N_SELECT = 16
N_LOCAL_SEL = 2
WINDOW = 512
TOP_K = 2
LN_EPS = 1e-5
NEG = -1e30
BIG = 1e30

Q_TILE = 128
KEY_TILE = 256
KV_COLS = N_KV_HEADS * HEAD_DIM
Q_COLS = N_HEADS * HEAD_DIM
LANES = 128
VMEM_LIMIT = 56 * 1024 * 1024


def _cparams(*sem):
    return pltpu.CompilerParams(dimension_semantics=sem, vmem_limit_bytes=VMEM_LIMIT)


def _round_up(a, m):
    return -(-a // m) * m


def _row_tile(n, pref):
    t = min(n, pref)
    while n % t:
        t //= 2
    assert t >= 8 and n % t == 0
    return t


def _sigmoid(v):
    return 1.0 / (1.0 + jnp.exp(-v))


def _silu(v):
    return v * _sigmoid(v)


def _layer_norm(v, g, b):
    mu = jnp.mean(v, axis=-1, keepdims=True)
    d = v - mu
    var = jnp.mean(d * d, axis=-1, keepdims=True)
    return d * lax.rsqrt(var + LN_EPS) * g + b


def _dot(a, b):
    return jnp.dot(a, b, preferred_element_type=F32)


def _split_bf16(v):
    hi = v.astype(BF16)
    lo = (v - hi.astype(F32)).astype(BF16)
    return hi, lo


def _pw1_glu_kernel(x_ref, wa_ref, wg_ref, ba_ref, bg_ref, u_ref):
    x = x_ref[...].astype(BF16)
    a = _dot(x, wa_ref[...]) + ba_ref[...]
    g = _dot(x, wg_ref[...]) + bg_ref[...]
    u_ref[...] = a * _sigmoid(g)


def _pw1_glu(x, wa, wg, ba, bg):
    n, d = x.shape
    tm = _row_tile(n, 512)
    full = lambda i: (0, 0)
    return pl.pallas_call(
        _pw1_glu_kernel,
        grid=(n // tm,),
        in_specs=[pl.BlockSpec((tm, d), lambda i: (i, 0)),
                  pl.BlockSpec(wa.shape, full), pl.BlockSpec(wg.shape, full),
                  pl.BlockSpec(ba.shape, full), pl.BlockSpec(bg.shape, full)],
        out_specs=pl.BlockSpec((tm, d), lambda i: (i, 0)),
        out_shape=jax.ShapeDtypeStruct((n, d), F32),
        compiler_params=_cparams("parallel"),
    )(x, wa, wg, ba, bg)


HALO = 32


def _dw_taps(ext_ref, wdw_ref, rows, width):
    off = HALO - (width - 1)
    acc = wdw_ref[0:1, :] * ext_ref[pl.ds(off, rows), :]
    for k in range(1, width):
        acc = acc + wdw_ref[k:k + 1, :] * ext_ref[pl.ds(off + k, rows), :]
    return acc


def _conv2_prompt_kernel(width, alpha, u_ref, halo_ref, x_ref, wdw_ref, bdw_ref, cg_ref, cb_ref,
                         w2_ref, b2_ref, g_ref, b_ref, o_ref, ext_ref):
    i = pl.program_id(1)
    tm = u_ref.shape[1]

    @pl.when(i == 0)
    def _():
        ext_ref[0:HALO, :] = jnp.zeros((HALO, ext_ref.shape[1]), F32)

    @pl.when(i > 0)
    def _():
        ext_ref[0:HALO, :] = halo_ref[0]

    ext_ref[HALO:, :] = u_ref[0]
    c = _dw_taps(ext_ref, wdw_ref, tm, width) + bdw_ref[...]
    c = _layer_norm(c, cg_ref[...], cb_ref[...])
    h = _dot(_silu(c).astype(BF16), w2_ref[...]) + b2_ref[...]
    o_ref[0] = _layer_norm(alpha * x_ref[0] + h, g_ref[...], b_ref[...])


def _conv2_prompt(u, x, wdw, bdw, cg, cb, w2, b2, g, b, alpha):
    bsz, t, d = u.shape
    width = wdw.shape[0]
    tm = _row_tile(t, 256)
    assert tm % HALO == 0 and width - 1 <= HALO
    r = tm // HALO
    full = lambda bi, i: (0, 0)
    vec = pl.BlockSpec((1, d), full)
    return pl.pallas_call(
        functools.partial(_conv2_prompt_kernel, width, alpha),
        grid=(bsz, t // tm),
        in_specs=[pl.BlockSpec((1, tm, d), lambda bi, i: (bi, i, 0)),
                  pl.BlockSpec((1, HALO, d), lambda bi, i: (bi, jnp.maximum(i * r - 1, 0), 0)),
                  pl.BlockSpec((1, tm, d), lambda bi, i: (bi, i, 0)),
                  pl.BlockSpec(wdw.shape, full), vec, vec, vec,
                  pl.BlockSpec(w2.shape, full), vec, vec, vec],
        out_specs=pl.BlockSpec((1, tm, d), lambda bi, i: (bi, i, 0)),
        out_shape=jax.ShapeDtypeStruct((bsz, t, d), F32),
        scratch_shapes=[pltpu.VMEM((HALO + tm, d), F32)],
        compiler_params=_cparams("parallel", "arbitrary"),
    )(u, u, x, wdw, bdw, cg, cb, w2, b2, g, b)


def _conv2_sample_kernel(width, alpha, ext_ref, x_ref, wdw_ref, bdw_ref, cg_ref, cb_ref,
                         w2_ref, b2_ref, g_ref, b_ref, o_ref):
    bsz, rows, d = x_ref.shape
    acc = wdw_ref[0:1, :][None] * ext_ref[:, pl.ds(0, rows), :]
    for k in range(1, width):
        acc = acc + wdw_ref[k:k + 1, :][None] * ext_ref[:, pl.ds(k, rows), :]
    c = acc.reshape(bsz * rows, d) + bdw_ref[...]
    c = _layer_norm(c, cg_ref[...], cb_ref[...])
    h = _dot(_silu(c).astype(BF16), w2_ref[...]) + b2_ref[...]
    x = x_ref[...].reshape(bsz * rows, d)
    o_ref[...] = _layer_norm(alpha * x + h, g_ref[...], b_ref[...]).reshape(bsz, rows, d)


def _conv2_sample(ext, x, wdw, bdw, cg, cb, w2, b2, g, b, alpha):
    bsz, rows, d = x.shape
    return pl.pallas_call(
        functools.partial(_conv2_sample_kernel, wdw.shape[0], alpha),
        out_shape=jax.ShapeDtypeStruct((bsz, rows, d), F32),
        compiler_params=pltpu.CompilerParams(vmem_limit_bytes=VMEM_LIMIT),
    )(ext, x, wdw, bdw, cg, cb, w2, b2, g, b)


def _ffn_kernel(alpha, x_ref, wg_ref, wu_ref, wd_ref, g_ref, b_ref, o_ref, acc_ref):
    k = pl.program_id(1)

    @pl.when(k == 0)
    def _():
        acc_ref[...] = jnp.zeros(acc_ref.shape, F32)

    x = x_ref[...].astype(BF16)
    h = _silu(_dot(x, wg_ref[...])) * _dot(x, wu_ref[...])
    acc_ref[...] += _dot(h.astype(BF16), wd_ref[...])

    @pl.when(k == pl.num_programs(1) - 1)
    def _():
        o_ref[...] = _layer_norm(alpha * x_ref[...] + acc_ref[...], g_ref[...], b_ref[...])


def _ff_tile(dff):
    for tf in (512, 384, 256, 128):
        if dff % tf == 0:
            return tf
    return dff


def _ffn_ln(x, wg, wu, wd, g, b, alpha):
    n, d = x.shape
    dff = wg.shape[1]
    tm = _row_tile(n, 1024)
    tf = _ff_tile(dff)
    vec = pl.BlockSpec((1, d), lambda i, k: (0, 0))
    return pl.pallas_call(
        functools.partial(_ffn_kernel, alpha),
        grid=(n // tm, dff // tf),
        in_specs=[pl.BlockSpec((tm, d), lambda i, k: (i, 0)),
                  pl.BlockSpec((d, tf), lambda i, k: (0, k)),
                  pl.BlockSpec((d, tf), lambda i, k: (0, k)),
                  pl.BlockSpec((tf, d), lambda i, k: (k, 0)), vec, vec],
        out_specs=pl.BlockSpec((tm, d), lambda i, k: (i, 0)),
        out_shape=jax.ShapeDtypeStruct((n, d), F32),
        scratch_shapes=[pltpu.VMEM((tm, d), F32)],
        compiler_params=_cparams("parallel", "arbitrary"),
    )(x, wg, wu, wd, g, b)


def _router_kernel(n_exp, x_ref, wh_ref, wl_ref, comb_ref):
    xh, xl = _split_bf16(x_ref[...])
    logits = _dot(xh, wh_ref[...]) + (_dot(xh, wl_ref[...]) + _dot(xl, wh_ref[...]))
    lane = lax.broadcasted_iota(jnp.int32, logits.shape, 1)
    logits = jnp.where(lane < n_exp, logits, -jnp.inf)
    m1 = jnp.max(logits, axis=-1, keepdims=True)
    i1 = jnp.min(jnp.where(logits == m1, lane, LANES), axis=-1, keepdims=True)
    rest = jnp.where(lane == i1, -jnp.inf, logits)
    m2 = jnp.max(rest, axis=-1, keepdims=True)
    i2 = jnp.min(jnp.where(rest == m2, lane, LANES), axis=-1, keepdims=True)
    e2 = jnp.exp(m2 - m1)
    w1 = 1.0 / (1.0 + e2)
    w2 = e2 / (1.0 + e2)
    comb_ref[...] = jnp.where(lane == i1, w1, 0.0) + jnp.where(lane == i2, w2, 0.0)


def _router(x, w_router):
    n, d = x.shape
    n_exp = w_router.shape[1]
    assert TOP_K == 2 and n_exp <= LANES
    wpad = jnp.pad(w_router, ((0, 0), (0, LANES - n_exp)))
    wh = wpad.astype(BF16)
    wl = (wpad - wh.astype(F32)).astype(BF16)
    tm = _row_tile(n, 512)
    return pl.pallas_call(
        functools.partial(_router_kernel, n_exp),
        grid=(n // tm,),
        in_specs=[pl.BlockSpec((tm, d), lambda i: (i, 0)),
                  pl.BlockSpec((d, LANES), lambda i: (0, 0)),
                  pl.BlockSpec((d, LANES), lambda i: (0, 0))],
        out_specs=pl.BlockSpec((tm, LANES), lambda i: (i, 0)),
        out_shape=jax.ShapeDtypeStruct((n, LANES), F32),
        compiler_params=_cparams("parallel"),
    )(x, wh, wl)


def _moe_kernel(alpha, x_ref, comb_ref, wg_ref, wu_ref, wd_ref, g_ref, b_ref, o_ref, acc_ref):
    e = pl.program_id(1)
    k = pl.program_id(2)

    @pl.when((e == 0) & (k == 0))
    def _():
        acc_ref[...] = jnp.zeros(acc_ref.shape, F32)

    comb = comb_ref[...]
    lane = lax.broadcasted_iota(jnp.int32, comb.shape, 1)
    c = jnp.sum(jnp.where(lane == e, comb, 0.0), axis=-1, keepdims=True)
    x = x_ref[...].astype(BF16)
    h = _silu(_dot(x, wg_ref[0])) * _dot(x, wu_ref[0])
    acc_ref[...] += c * _dot(h.astype(BF16), wd_ref[0])

    @pl.when((e == pl.num_programs(1) - 1) & (k == pl.num_programs(2) - 1))
    def _():
        o_ref[...] = _layer_norm(alpha * x_ref[...] + acc_ref[...], g_ref[...], b_ref[...])


def _moe_ln(x, comb, wg, wu, wd, g, b, alpha):
    n, d = x.shape
    n_exp, _, dff = wg.shape
    tm = _row_tile(n, 1024)
    tf = _ff_tile(dff)
    vec = pl.BlockSpec((1, d), lambda i, e, k: (0, 0))
    return pl.pallas_call(
        functools.partial(_moe_kernel, alpha),
        grid=(n // tm, n_exp, dff // tf),
        in_specs=[pl.BlockSpec((tm, d), lambda i, e, k: (i, 0)),
                  pl.BlockSpec((tm, LANES), lambda i, e, k: (i, 0)),
                  pl.BlockSpec((1, d, tf), lambda i, e, k: (e, 0, k)),
                  pl.BlockSpec((1, d, tf), lambda i, e, k: (e, 0, k)),
                  pl.BlockSpec((1, tf, d), lambda i, e, k: (e, k, 0)), vec, vec],
        out_specs=pl.BlockSpec((tm, d), lambda i, e, k: (i, 0)),
        out_shape=jax.ShapeDtypeStruct((n, d), F32),
        scratch_shapes=[pltpu.VMEM((tm, d), F32)],
        compiler_params=_cparams("parallel", "arbitrary", "arbitrary"),
    )(x, comb, wg, wu, wd, g, b)


def _rope_chunk(v, cos, sa, sb):
    half = ROT_DIM // 2
    return v * cos + pltpu.roll(v, LANES - half, 1) * sa + pltpu.roll(v, half, 1) * sb


def _nsa_proj_kernel(x_ref, w_ref, cos_ref, sa_ref, sb_ref,
                     q_ref, kc_ref, vc_ref, ks_ref, vs_ref, kw_ref, vw_ref, gate_ref):
    y = _dot(x_ref[...].astype(BF16), w_ref[...])
    cos, sa, sb = cos_ref[...], sa_ref[...], sb_ref[...]
    scale = HEAD_DIM ** -0.5
    for j in range(Q_COLS // LANES):
        q_ref[:, j * LANES:(j + 1) * LANES] = (
            _rope_chunk(y[:, j * LANES:(j + 1) * LANES], cos, sa, sb) * scale).astype(BF16)
    base = Q_COLS
    for idx, (ref, roped) in enumerate(((kc_ref, False), (vc_ref, False), (ks_ref, True),
                                        (vs_ref, False), (kw_ref, True), (vw_ref, False))):
        for j in range(KV_COLS // LANES):
            lo = base + idx * KV_COLS + j * LANES
            v = y[:, lo:lo + LANES]
            ref[:, j * LANES:(j + 1) * LANES] = _rope_chunk(v, cos, sa, sb) if roped else v
    gate_ref[...] = _sigmoid(y[:, base + 6 * KV_COLS:base + 6 * KV_COLS + LANES])


def _rope_tables(pos):
    half = ROT_DIM // 2
    inv_freq = ROPE_THETA ** (-jnp.arange(half, dtype=F32) / half)
    ang = pos.astype(F32)[:, None] * inv_freq[None, :]
    cos, sin = jnp.cos(ang), jnp.sin(ang)
    n = pos.shape[0]
    rest = HEAD_DIM - ROT_DIM
    one = jnp.ones((n, rest), F32)
    zero = jnp.zeros((n, rest), F32)
    zh = jnp.zeros((n, half), F32)
    c = jnp.concatenate([cos, cos, one], axis=1)
    sa = jnp.concatenate([-sin, zh, zero], axis=1)
    sb = jnp.concatenate([zh, sin, zero], axis=1)
    rep = LANES // HEAD_DIM
    return tuple(jnp.tile(t, (1, rep)) for t in (c, sa, sb))


def _nsa_proj(x, w_pad, tables, n_tab_blocks):
    n, d = x.shape
    tm = tables[0].shape[0] // n_tab_blocks
    assert n % tm == 0
    row = lambda i: (i, 0)
    tab = pl.BlockSpec((tm, LANES), lambda i: (i % n_tab_blocks, 0))
    kv_spec = pl.BlockSpec((tm, KV_COLS), row)
    kv_shape = jax.ShapeDtypeStruct((n, KV_COLS), F32)
    return pl.pallas_call(
        _nsa_proj_kernel,
        grid=(n // tm,),
        in_specs=[pl.BlockSpec((tm, d), row), pl.BlockSpec(w_pad.shape, lambda i: (0, 0)), tab, tab, tab],
        out_specs=[pl.BlockSpec((tm, Q_COLS), row)] + [kv_spec] * 6 + [pl.BlockSpec((tm, LANES), row)],
        out_shape=[jax.ShapeDtypeStruct((n, Q_COLS), BF16)] + [kv_shape] * 6
                  + [jax.ShapeDtypeStruct((n, LANES), F32)],
        compiler_params=_cparams("parallel"),
    )(x, w_pad, *tables)


def _compress_kernel(roped, g_ref, pea_ref, peb_ref, wa_ref, wb_ref, w2_ref, cos_ref, sa_ref, sb_ref, o_ref):
    grp = g_ref[0]
    rows = grp.shape[0]
    a = _dot((grp + pea_ref[...]).astype(BF16), wa_ref[...])
    bm = _dot((grp + peb_ref[...]).astype(BF16), wb_ref[...])
    h = a + pltpu.roll(bm, rows - 1, 0)
    out = _dot(_silu(h).astype(BF16), w2_ref[...])
    if roped:
        cos, sa, sb = cos_ref[...], sa_ref[...], sb_ref[...]
        for j in range(KV_COLS // LANES):
            o_ref[0, :, j * LANES:(j + 1) * LANES] = _rope_chunk(out[:, j * LANES:(j + 1) * LANES], cos, sa, sb)
    else:
        o_ref[0] = out


def _compress(groups, pe, w1, w2, roped):
    bsz, ng, gw = groups.shape
    assert CMP_LEN == 2 * CMP_STRIDE and gw == CMP_STRIDE * KV_COLS
    pea, peb, wa, wb, w2b = _compress_weights(pe, w1, w2)
    c_end = jnp.arange(ng) * CMP_STRIDE + (CMP_LEN - 1)
    tables = _rope_tables(c_end)
    full = lambda b: (0, 0)
    tab = pl.BlockSpec((ng, LANES), full)
    return pl.pallas_call(
        functools.partial(_compress_kernel, roped),
        grid=(bsz,),
        in_specs=[pl.BlockSpec((1, ng, gw), lambda b: (b, 0, 0)),
                  pl.BlockSpec((1, gw), full), pl.BlockSpec((1, gw), full),
                  pl.BlockSpec(wa.shape, full), pl.BlockSpec(wb.shape, full), pl.BlockSpec(w2b.shape, full),
                  tab, tab, tab],
        out_specs=pl.BlockSpec((1, ng, KV_COLS), lambda b: (b, 0, 0)),
        out_shape=jax.ShapeDtypeStruct((bsz, ng, KV_COLS), F32),
        compiler_params=_cparams("parallel"),
    )(groups, pea, peb, wa, wb, w2b, *tables)


def _softmax_cols(s, mask):
    s = jnp.where(mask, s, NEG)
    m = jnp.max(s, axis=0, keepdims=True)
    p = jnp.where(mask, jnp.exp(s - m), 0.0)
    return p * (1.0 / jnp.maximum(jnp.sum(p, axis=0, keepdims=True), 1e-30))


def _attn_kernel(qt0, n_cmp, n_blk, w_base, qT_ref, kc_ref, vcT_ref, kaug_ref, vsT_ref, kw_ref, vwT_ref,
                 gate_ref, ovT_ref, o_ref):
    qt = pl.program_id(2) + qt0
    q0 = qt * Q_TILE
    qT = qT_ref[0, 0, 0]
    cols = qT.shape[1]
    qpos = q0 + lax.broadcasted_iota(jnp.int32, (1, cols), 1) % Q_TILE
    nb_pad = ovT_ref.shape[0]

    kc = kc_ref[0, 0]
    ncp = kc.shape[0]
    cidx = lax.broadcasted_iota(jnp.int32, (ncp, 1), 0)
    c_end = cidx * CMP_STRIDE + (CMP_LEN - 1)
    p_c = _softmax_cols(_dot(kc, qT), (c_end <= qpos) & (cidx < n_cmp))
    o_c = _dot(vcT_ref[0, 0], p_c.astype(BF16))

    p_sum = p_c[:, 0:Q_TILE]
    for g in range(1, GROUP):
        p_sum = p_sum + p_c[:, g * Q_TILE:(g + 1) * Q_TILE]
    p_hi, p_lo = _split_bf16(p_sum)
    imp = _dot(ovT_ref[...], p_hi) + _dot(ovT_ref[...], p_lo)

    blk = lax.broadcasted_iota(jnp.int32, (nb_pad, 1), 0)
    qp = qpos[:, 0:Q_TILE]
    lag = qp // SEL_BLOCK - blk
    forced = (blk == 0) | ((lag >= 0) & (lag < N_LOCAL_SEL))
    score = jnp.where(forced, BIG, jnp.where(blk * SEL_BLOCK <= qp, imp, -BIG))
    score = jnp.where(blk < n_blk, score, -jnp.inf)
    bias = jnp.full(score.shape, NEG, F32)
    for _ in range(min(N_SELECT, n_blk)):
        m = jnp.max(score, axis=0, keepdims=True)
        j = jnp.min(jnp.where(score == m, blk, nb_pad), axis=0, keepdims=True)
        hit = blk == j
        bias = jnp.where(hit, 0.0, bias)
        score = jnp.where(hit, -jnp.inf, score)
    bias = bias.astype(BF16)
    q_aug = jnp.concatenate([jnp.concatenate([bias] * GROUP, axis=1), qT], axis=0)

    def body(kb, carry):
        m, l, acc = carry
        k0 = pl.multiple_of(kb * KEY_TILE, KEY_TILE)
        s = _dot(kaug_ref[0, 0, pl.ds(k0, KEY_TILE), :], q_aug)
        kpos = k0 + lax.broadcasted_iota(jnp.int32, (KEY_TILE, 1), 0)
        s = jnp.where(kpos <= qpos, s, NEG)
        m_new = jnp.maximum(m, jnp.max(s, axis=0, keepdims=True))
        alpha = jnp.exp(m - m_new)
        p = jnp.exp(s - m_new)
        l = alpha * l + jnp.sum(p, axis=0, keepdims=True)
        acc = alpha * acc + _dot(vsT_ref[0, 0, :, pl.ds(k0, KEY_TILE)], p.astype(BF16))
        return m_new, l, acc

    n_kb = (q0 + Q_TILE + KEY_TILE - 1) // KEY_TILE
    init = (jnp.full((1, cols), NEG, F32), jnp.zeros((1, cols), F32), jnp.zeros((HEAD_DIM, cols), F32))
    _, l_s, acc_s = lax.fori_loop(0, n_kb, body, init)
    o_s = acc_s / jnp.maximum(l_s, 1e-30)

    wlen = WINDOW + Q_TILE
    w0 = pl.multiple_of(jnp.maximum(q0 - WINDOW, w_base) - w_base, Q_TILE)
    wpos = w0 + w_base + lax.broadcasted_iota(jnp.int32, (wlen, 1), 0)
    p_w = _softmax_cols(_dot(kw_ref[0, 0, pl.ds(w0, wlen), :], qT),
                        (wpos <= qpos) & (wpos > qpos - WINDOW))
    o_w = _dot(vwT_ref[0, 0, :, pl.ds(w0, wlen)], p_w.astype(BF16))

    gate = gate_ref[0, 0, 0]
    o_ref[0, 0, 0] = (gate[0:1] * o_c + gate[1:2] * o_s + gate[2:3] * o_w).astype(BF16)


def _overlap_t(n_cmp, n_cmp_pad, n_blk, nb_pad):
    cs = np.arange(n_cmp_pad)[None, :] * CMP_STRIDE
    ss = np.arange(nb_pad)[:, None] * SEL_BLOCK
    ov = np.maximum(np.minimum(cs + CMP_LEN, ss + SEL_BLOCK) - np.maximum(cs, ss), 0) / CMP_LEN
    ov = ov * (np.arange(n_cmp_pad)[None, :] < n_cmp) * (np.arange(nb_pad)[:, None] < n_blk)
    return jnp.asarray(ov, BF16)


def _nsa_attention(q, kc, vc, ks, vs, kw, vw, gates, *, t_total, qt0, n_qt, w_base):
    bsz = q.shape[0]
    tk = ks.shape[1]
    n_cmp = (t_total - CMP_LEN) // CMP_STRIDE + 1
    n_blk = -(-t_total // SEL_BLOCK)
    nb_pad = _round_up(tk // SEL_BLOCK, 16)
    ncp = kc.shape[1]
    cols = GROUP * Q_TILE
    assert tk % KEY_TILE == 0 and tk >= _round_up((qt0 + n_qt) * Q_TILE, KEY_TILE) and n_cmp <= ncp

    heads = lambda a: a.astype(BF16).reshape(bsz, a.shape[1], N_KV_HEADS, HEAD_DIM).transpose(0, 2, 1, 3)
    heads_t = lambda a: a.astype(BF16).reshape(bsz, a.shape[1], N_KV_HEADS, HEAD_DIM).transpose(0, 2, 3, 1)
    qT = q.reshape(bsz, n_qt, Q_TILE, N_KV_HEADS, GROUP, HEAD_DIM).transpose(0, 3, 1, 5, 4, 2)
    qT = qT.reshape(bsz, N_KV_HEADS, n_qt, HEAD_DIM, cols)
    onehot = jnp.asarray(np.arange(tk)[:, None] // SEL_BLOCK == np.arange(nb_pad)[None, :], BF16)
    kaug = jnp.concatenate([jnp.broadcast_to(onehot, (bsz, N_KV_HEADS, tk, nb_pad)), heads(ks)], axis=-1)
    gT = gates[..., :3 * N_HEADS].reshape(bsz, n_qt, Q_TILE, N_KV_HEADS, GROUP, 3).transpose(0, 3, 1, 5, 4, 2)
    gT = jnp.pad(gT.reshape(bsz, N_KV_HEADS, n_qt, 3, cols), ((0, 0), (0, 0), (0, 0), (0, 5), (0, 0)))
    ovT = _overlap_t(n_cmp, ncp, n_blk, nb_pad)
    lw = kw.shape[1]

    per_head = lambda *shape: pl.BlockSpec((1, 1) + shape, lambda b, h, i: (b, h, 0, 0))
    per_tile = lambda *shape: pl.BlockSpec((1, 1, 1) + shape, lambda b, h, i: (b, h, i, 0, 0))
    oT = pl.pallas_call(
        functools.partial(_attn_kernel, qt0, n_cmp, n_blk, w_base),
        grid=(bsz, N_KV_HEADS, n_qt),
        in_specs=[per_tile(HEAD_DIM, cols),
                  per_head(ncp, HEAD_DIM), per_head(HEAD_DIM, ncp),
                  per_head(tk, nb_pad + HEAD_DIM), per_head(HEAD_DIM, tk),
                  per_head(lw, HEAD_DIM), per_head(HEAD_DIM, lw),
                  per_tile(8, cols),
                  pl.BlockSpec(ovT.shape, lambda b, h, i: (0, 0))],
        out_specs=per_tile(HEAD_DIM, cols),
        out_shape=jax.ShapeDtypeStruct((bsz, N_KV_HEADS, n_qt, HEAD_DIM, cols), BF16),
        compiler_params=_cparams("parallel", "parallel", "arbitrary"),
    )(qT, heads(kc), heads_t(vc), kaug, heads_t(vs), heads(kw), heads_t(vw), gT, ovT)
    o = oT.reshape(bsz, N_KV_HEADS, n_qt, HEAD_DIM, GROUP, Q_TILE).transpose(0, 2, 5, 1, 4, 3)
    return o.reshape(bsz, n_qt * Q_TILE, Q_COLS)


def _paged_pipeline(pt_ref, pools, bufs, sems, pages_per_chunk, extra_pages, rows_per_page):
    seq, chunk = pl.program_id(0), pl.program_id(1)
    n_chunk = pl.num_programs(1)
    step = seq * n_chunk + chunk
    total = pl.num_programs(0) * n_chunk
    slot = step % 2
    last_page = n_chunk * pages_per_chunk - 1
    n_copy = pages_per_chunk + extra_pages

    def start(sq, ck, sl):
        def body(j, carry):
            page = pt_ref[sq, jnp.minimum(ck * pages_per_chunk + j, last_page)]
            dst = pl.ds(pl.multiple_of(j * rows_per_page, rows_per_page), rows_per_page)
            for i, (pool, buf) in enumerate(zip(pools, bufs)):
                pltpu.make_async_copy(pool.at[page], buf.at[sl, dst], sems.at[i, sl]).start()
            return carry
        lax.fori_loop(0, n_copy, body, 0)

    @pl.when(step == 0)
    def _():
        start(seq, chunk, slot)

    @pl.when(step + 1 < total)
    def _():
        nxt = step + 1
        start(nxt // n_chunk, nxt % n_chunk, 1 - slot)

    def wait_body(j, carry):
        dst = pl.ds(pl.multiple_of(j * rows_per_page, rows_per_page), rows_per_page)
        for i, (pool, buf) in enumerate(zip(pools, bufs)):
            pltpu.make_async_copy(pool.at[0], buf.at[slot, dst], sems.at[i, slot]).wait()
        return carry
    lax.fori_loop(0, n_copy, wait_body, 0)
    return slot


def _compress_paged_kernel(pages_per_chunk, n_cmp, pt_ref, poolk_ref, poolv_ref,
                           peak_ref, pebk_ref, wak_ref, wbk_ref, w2k_ref,
                           peav_ref, pebv_ref, wav_ref, wbv_ref, w2v_ref,
                           cos_ref, sa_ref, sb_ref, ok_ref, ov_ref, kbuf, vbuf, sems):
    gpp = poolk_ref.shape[1]
    rows = pages_per_chunk * gpp
    slot = _paged_pipeline(pt_ref, (poolk_ref, poolv_ref), (kbuf, vbuf), sems, pages_per_chunk, 1, gpp)
    row0 = pl.program_id(1) * rows
    valid = row0 + lax.broadcasted_iota(jnp.int32, (rows, 1), 0) < n_cmp

    def mlp(buf, pea_ref, peb_ref, wa_ref, wb_ref, w2_ref):
        grp = buf[slot]
        a = _dot((grp[:rows] + pea_ref[...]).astype(BF16), wa_ref[...])
        bm = _dot((grp + peb_ref[...]).astype(BF16), wb_ref[...])
        h = a + pltpu.roll(bm, rows + gpp - 1, 0)[:rows]
        return jnp.where(valid, _dot(_silu(h).astype(BF16), w2_ref[...]), 0.0)

    outk = mlp(kbuf, peak_ref, pebk_ref, wak_ref, wbk_ref, w2k_ref)
    cos, sa, sb = cos_ref[...], sa_ref[...], sb_ref[...]
    for j in range(KV_COLS // LANES):
        ok_ref[0, :, j * LANES:(j + 1) * LANES] = _rope_chunk(
            outk[:, j * LANES:(j + 1) * LANES], cos, sa, sb).astype(BF16)
    ov_ref[0] = mlp(vbuf, peav_ref, pebv_ref, wav_ref, wbv_ref, w2v_ref).astype(BF16)


def _compress_weights(pe, w1, w2):
    eye = jnp.eye(N_KV_HEADS, dtype=F32)
    blockdiag = lambda w: jnp.einsum('lde,hg->lhdge', w, eye).reshape(-1, KV_COLS).astype(BF16)
    w2b = jnp.einsum('de,hg->hdge', w2, eye).reshape(KV_COLS, KV_COLS).astype(BF16)
    pe_row = lambda p: jnp.tile(p[:, None, :], (1, N_KV_HEADS, 1)).reshape(1, -1)
    return (pe_row(pe[:CMP_STRIDE]), pe_row(pe[CMP_STRIDE:]),
            blockdiag(w1[:CMP_STRIDE]), blockdiag(w1[CMP_STRIDE:]), w2b)


def _compress_paged(pool_k, pool_v, page_table, pe, w1, w2, n_cmp):
    bd, n_pages = page_table.shape
    page = pool_k.shape[1]
    gpp = page // CMP_STRIDE
    gw = CMP_STRIDE * KV_COLS
    n_chunk = 2 if n_pages % 2 == 0 else 1
    ppc = n_pages // n_chunk
    rows = ppc * gpp
    ng = n_pages * gpp
    as_groups = lambda pool: pool.reshape(pool.shape[0], gpp, gw)
    wk = _compress_weights(pe[0], w1[0], w2[0])
    wv = _compress_weights(pe[1], w1[1], w2[1])
    tables = _rope_tables(jnp.arange(ng) * CMP_STRIDE + (CMP_LEN - 1))
    full = lambda b, c, pt: (0, 0)
    wspecs = [pl.BlockSpec(w.shape, full) for w in wk + wv]
    tab = pl.BlockSpec((rows, LANES), lambda b, c, pt: (c, 0))
    out_spec = pl.BlockSpec((1, rows, KV_COLS), lambda b, c, pt: (b, c, 0))
    out_shape = jax.ShapeDtypeStruct((bd, ng, KV_COLS), BF16)
    return pl.pallas_call(
        functools.partial(_compress_paged_kernel, ppc, n_cmp),
        grid_spec=pltpu.PrefetchScalarGridSpec(
            num_scalar_prefetch=1, grid=(bd, n_chunk),
            in_specs=[pl.BlockSpec(memory_space=pl.ANY), pl.BlockSpec(memory_space=pl.ANY)] + wspecs + [tab] * 3,
            out_specs=[out_spec, out_spec],
            scratch_shapes=[pltpu.VMEM((2, rows + gpp, gw), F32), pltpu.VMEM((2, rows + gpp, gw), F32),
                            pltpu.SemaphoreType.DMA((2, 2))]),
        out_shape=[out_shape, out_shape],
        compiler_params=_cparams("arbitrary", "arbitrary"),
    )(page_table, as_groups(pool_k), as_groups(pool_v), *wk, *wv, *tables)


NEW_PAD = 128
CHUNK_PAGES = 16


def _dot_nt(a, b):
    return lax.dot_general(a, b, (((1,), (1,)), ((), ())), preferred_element_type=F32)


def _softmax_rows2(s1, mask1, s2, mask2):
    s1, s2 = jnp.where(mask1, s1, NEG), jnp.where(mask2, s2, NEG)
    m = jnp.maximum(jnp.max(s1, axis=-1, keepdims=True), jnp.max(s2, axis=-1, keepdims=True))
    p1 = jnp.where(mask1, jnp.exp(s1 - m), 0.0)
    p2 = jnp.where(mask2, jnp.exp(s2 - m), 0.0)
    r = 1.0 / jnp.maximum(jnp.sum(p1, axis=-1, keepdims=True) + jnp.sum(p2, axis=-1, keepdims=True), 1e-30)
    return p1 * r, p2 * r


def _attn_sample_kernel(s_new, past, n_cmp, n_blk, pt_ref, poolk_ref, poolv_ref, q_ref, kc_ref, vc_ref,
                        knew_ref, vnew_ref, kwb_ref, vwb_ref, kwn_ref, vwn_ref, gate_ref,
                        et_ref, ett_ref, ov_ref, gsum_ref, o_ref,
                        kbuf, vbuf, sems, m_ref, l_ref, acc_ref, bias_ref, part_ref):
    page = poolk_ref.shape[1]
    slot = _paged_pipeline(pt_ref, (poolk_ref, poolv_ref), (kbuf, vbuf), sems, CHUNK_PAGES, 0, page)
    chunk = pl.program_id(1)
    q = q_ref[0]
    n_rows = q.shape[0]
    qpos = past + lax.broadcasted_iota(jnp.int32, (n_rows, 1), 0) % s_new
    gate = gate_ref[0]

    @pl.when(chunk == 0)
    def _():
        ncp = kc_ref.shape[1]
        cidx = lax.broadcasted_iota(jnp.int32, (1, ncp), 1)
        cmask = (cidx * CMP_STRIDE + (CMP_LEN - 1) <= qpos) & (cidx < n_cmp)
        s_c = jnp.where(cmask, _dot_nt(q, kc_ref[0]), NEG)
        p_c = jnp.where(cmask, jnp.exp(s_c - jnp.max(s_c, axis=-1, keepdims=True)), 0.0)
        p_c = p_c * (1.0 / jnp.maximum(jnp.sum(p_c, axis=-1, keepdims=True), 1e-30))
        o_c = _dot(p_c.astype(BF16), vc_ref[0])
        p_hi, p_lo = _split_bf16(p_c)
        p_sum = _dot(gsum_ref[...], p_hi) + _dot(gsum_ref[...], p_lo)
        s_hi, s_lo = _split_bf16(p_sum)
        imp = _dot(s_hi, ov_ref[...]) + _dot(s_lo, ov_ref[...])
        nb_pad = imp.shape[1]
        blk = lax.broadcasted_iota(jnp.int32, (1, nb_pad), 1)
        lag = qpos // SEL_BLOCK - blk
        forced = (blk == 0) | ((lag >= 0) & (lag < N_LOCAL_SEL))
        score = jnp.where(forced, BIG, jnp.where(blk * SEL_BLOCK <= qpos, imp, -BIG))
        score = jnp.where(blk < n_blk, score, -jnp.inf)
        bias = jnp.full(score.shape, NEG, F32)
        for _ in range(min(N_SELECT, n_blk)):
            m = jnp.max(score, axis=-1, keepdims=True)
            j = jnp.min(jnp.where(score == m, blk, nb_pad), axis=-1, keepdims=True)
            hit = blk == j
            bias = jnp.where(hit, 0.0, bias)
            score = jnp.where(hit, -jnp.inf, score)
        bias_ref[...] = bias
        w_buf = kwb_ref.shape[1]
        wpos = past - w_buf + lax.broadcasted_iota(jnp.int32, (1, w_buf), 1)
        npos = lax.broadcasted_iota(jnp.int32, (1, NEW_PAD), 1)
        p1, p2 = _softmax_rows2(
            _dot_nt(q, kwb_ref[0].astype(BF16)), (wpos <= qpos) & (wpos > qpos - WINDOW),
            _dot_nt(q, kwn_ref[0]), (past + npos <= qpos) & (past + npos > qpos - WINDOW) & (npos < s_new))
        o_w = _dot(p1.astype(BF16), vwb_ref[0].astype(BF16)) + _dot(p2.astype(BF16), vwn_ref[0])
        part_ref[...] = gate[:, 0:1] * o_c + gate[:, 2:3] * o_w
        m_ref[...] = jnp.full(m_ref.shape, NEG, F32)
        l_ref[...] = jnp.zeros(l_ref.shape, F32)
        acc_ref[...] = jnp.zeros(acc_ref.shape, F32)

    bias16 = bias_ref[...].astype(BF16)

    def flash(s, v):
        m_new = jnp.maximum(m_ref[...], jnp.max(s, axis=-1, keepdims=True))
        alpha = jnp.exp(m_ref[...] - m_new)
        p = jnp.exp(s - m_new)
        l_ref[...] = alpha * l_ref[...] + jnp.sum(p, axis=-1, keepdims=True)
        acc_ref[...] = alpha * acc_ref[...] + _dot(p.astype(BF16), v)
        m_ref[...] = m_new

    flash(_dot_nt(q, kbuf[slot].astype(BF16)) + _dot(bias16, et_ref[...]), vbuf[slot].astype(BF16))

    @pl.when(chunk == pl.num_programs(1) - 1)
    def _():
        npos = lax.broadcasted_iota(jnp.int32, (1, NEW_PAD), 1)
        s_t = _dot_nt(q, knew_ref[0]) + _dot(bias16, ett_ref[...])
        flash(jnp.where((past + npos <= qpos) & (npos < s_new), s_t, NEG), vnew_ref[0])
        o_s = acc_ref[...] * (1.0 / jnp.maximum(l_ref[...], 1e-30))
        o_ref[0] = part_ref[...] + gate[:, 1:2] * o_s


def _attn_sample(qs, kc, vc, pool_k, pool_v, page_table, k_new, v_new, kw_buf, vw_buf, kw_new, vw_new, gates,
                 *, s_new):
    bd, n_pages = page_table.shape
    page = pool_k.shape[1]
    past = n_pages * page
    w_buf = kw_buf.shape[1]
    n_rows = N_HEADS * s_new
    n_cmp = (past + s_new - CMP_LEN) // CMP_STRIDE + 1
    n_blk = -(-(past + s_new) // SEL_BLOCK)
    nb_pad = _round_up(n_blk, 2 * LANES)
    ncp = kc.shape[1]
    assert n_pages % CHUNK_PAGES == 0 and s_new <= NEW_PAD and n_cmp <= ncp and w_buf == min(WINDOW, past)
    ck = CHUNK_PAGES * page
    eye = jnp.eye(N_KV_HEADS, dtype=BF16)

    q5 = qs.reshape(bd, s_new, N_KV_HEADS, GROUP, HEAD_DIM).transpose(0, 2, 3, 1, 4)
    qbig = jnp.einsum('bkgsd,kj->bkgsjd', q5, eye).reshape(bd, n_rows, KV_COLS)
    g5 = gates[:, :3 * N_HEADS].reshape(bd, s_new, N_KV_HEADS, GROUP, 3).transpose(0, 2, 3, 1, 4)
    gate = jnp.pad(g5.reshape(bd, n_rows, 3), ((0, 0), (0, 0), (0, LANES - 3)))
    newrows = lambda a: jnp.pad(a.reshape(bd, s_new, KV_COLS), ((0, 0), (0, NEW_PAD - s_new), (0, 0))).astype(BF16)
    flat = lambda a: a.reshape(a.shape[0], a.shape[1], KV_COLS)

    keyblk = np.arange(past + NEW_PAD)[None, :] // SEL_BLOCK
    et = jnp.asarray(keyblk == np.arange(nb_pad)[:, None], BF16)
    cs = np.arange(ncp)[:, None] * CMP_STRIDE
    ss = np.arange(nb_pad)[None, :] * SEL_BLOCK
    ov = np.maximum(np.minimum(cs + CMP_LEN, ss + SEL_BLOCK) - np.maximum(cs, ss), 0) / CMP_LEN
    ov = jnp.asarray(ov * (np.arange(ncp)[:, None] < n_cmp) * (np.arange(nb_pad)[None, :] < n_blk), BF16)
    r = np.arange(n_rows)
    same = (r[:, None] // (GROUP * s_new) == r[None, :] // (GROUP * s_new)) & (r[:, None] % s_new == r[None, :] % s_new)
    gsum = jnp.asarray(same, BF16)

    per_seq = lambda a: pl.BlockSpec((1,) + a.shape[1:], lambda b, c, pt: (b, 0, 0))
    const = lambda a: pl.BlockSpec(a.shape, lambda b, c, pt: (0, 0))
    operands = [qbig, kc, vc, newrows(k_new), newrows(v_new), flat(kw_buf), flat(vw_buf),
                newrows(kw_new), newrows(vw_new), gate]
    o_big = pl.pallas_call(
        functools.partial(_attn_sample_kernel, s_new, past, n_cmp, n_blk),
        grid_spec=pltpu.PrefetchScalarGridSpec(
            num_scalar_prefetch=1, grid=(bd, n_pages // CHUNK_PAGES),
            in_specs=[pl.BlockSpec(memory_space=pl.ANY), pl.BlockSpec(memory_space=pl.ANY)]
                     + [per_seq(a) for a in operands]
                     + [pl.BlockSpec((nb_pad, ck), lambda b, c, pt: (0, c)),
                        pl.BlockSpec((nb_pad, NEW_PAD), lambda b, c, pt: (0, past // NEW_PAD)),
                        const(ov), const(gsum)],
            out_specs=pl.BlockSpec((1, n_rows, KV_COLS), lambda b, c, pt: (b, 0, 0)),
            scratch_shapes=[pltpu.VMEM((2, ck, KV_COLS), F32), pltpu.VMEM((2, ck, KV_COLS), F32),
                            pltpu.SemaphoreType.DMA((2, 2)),
                            pltpu.VMEM((n_rows, 1), F32), pltpu.VMEM((n_rows, 1), F32),
                            pltpu.VMEM((n_rows, KV_COLS), F32), pltpu.VMEM((n_rows, nb_pad), F32),
                            pltpu.VMEM((n_rows, KV_COLS), F32)]),
        out_shape=jax.ShapeDtypeStruct((bd, n_rows, KV_COLS), F32),
        compiler_params=_cparams("arbitrary", "arbitrary"),
    )(page_table, flat(pool_k), flat(pool_v), *operands, et, et, ov, gsum)
    o6 = o_big.reshape(bd, N_KV_HEADS, GROUP, s_new, N_KV_HEADS, HEAD_DIM)
    own = jnp.eye(N_KV_HEADS, dtype=F32)[None, :, None, None, :, None]
    o = jnp.sum(o6 * own, axis=4).transpose(0, 3, 1, 2, 4)
    return o.reshape(bd * s_new, Q_COLS).astype(BF16)


def _proj_ln_kernel(alpha, a_ref, w_ref, x_ref, g_ref, b_ref, o_ref):
    h = _dot(a_ref[...], w_ref[...])
    o_ref[...] = _layer_norm(alpha * x_ref[...] + h, g_ref[...], b_ref[...])


def _proj_ln(a, w, x, g, b, alpha):
    n, d = x.shape
    tm = _row_tile(n, 512)
    vec = pl.BlockSpec((1, d), lambda i: (0, 0))
    return pl.pallas_call(
        functools.partial(_proj_ln_kernel, alpha),
        grid=(n // tm,),
        in_specs=[pl.BlockSpec((tm, a.shape[1]), lambda i: (i, 0)), pl.BlockSpec(w.shape, lambda i: (0, 0)),
                  pl.BlockSpec((tm, d), lambda i: (i, 0)), vec, vec],
        out_specs=pl.BlockSpec((tm, d), lambda i: (i, 0)),
        out_shape=jax.ShapeDtypeStruct((n, d), F32),
        compiler_params=_cparams("parallel"),
    )(a, w, x, g, b)


def _conv_layer(yp, ys, state, w_pw1, b_pw1, w_dw, b_dw, cg, cb, w_pw2, b_pw2, g, b, alpha):
    bsz, t, d = yp.shape
    bd, s, _ = ys.shape
    width = w_dw.shape[0]
    wa, wg = w_pw1[:, :d].astype(BF16), w_pw1[:, d:].astype(BF16)
    ba, bg = b_pw1[None, :d], b_pw1[None, d:]
    w2 = w_pw2.astype(BF16)
    row = lambda v: v[None, :]
    tail = (w_dw, row(b_dw), row(cg), row(cb), w2, row(b_pw2), row(g), row(b), alpha)

    u_p = _pw1_glu(yp.reshape(bsz * t, d), wa, wg, ba, bg).reshape(bsz, t, d)
    new_p = _conv2_prompt(u_p, yp, *tail)
    st_p = u_p[:, t - (width - 1):]

    u_s = _pw1_glu(ys.reshape(bd * s, d), wa, wg, ba, bg).reshape(bd, s, d)
    ext = jnp.concatenate([state, u_s], axis=1)
    rows = _round_up(s, 8)
    ext_pad = jnp.pad(ext, ((0, 0), (0, _round_up(width - 1 + rows, 8) - ext.shape[1]), (0, 0)))
    xs_pad = jnp.pad(ys, ((0, 0), (0, rows - s), (0, 0)))
    new_s = _conv2_sample(ext_pad, xs_pad, *tail)[:, :s]
    st_s = ext[:, s:]
    return new_p, new_s, st_p, st_s


def _nsa_layer(yp, ys, pool_kc, pool_vc, pool_ks, pool_vs, kw_buf, vw_buf, page_table,
               w_in, pe, w1, w2, w_o, g, b, alpha):
    bsz, t, d = yp.shape
    bd, s, _ = ys.shape
    page = pool_kc.shape[1]
    past = page_table.shape[1] * page
    w_buf = kw_buf.shape[1]
    in_cols = w_in.shape[1]
    w_pad = jnp.pad(w_in, ((0, 0), (0, Q_COLS + 6 * KV_COLS + LANES - in_cols))).astype(BF16)
    w_o16 = w_o.astype(BF16)
    g, b = g[None, :], b[None, :]
    kv4 = lambda a, nb, nt: a.reshape(nb, nt, N_KV_HEADS, HEAD_DIM)

    tm = _row_tile(t, 256)
    tabs = _rope_tables(jnp.arange(t))
    q, k_c, v_c, k_s, v_s, k_w, v_w, gates = _nsa_proj(yp.reshape(bsz * t, d), w_pad, tabs, t // tm)
    b3 = lambda a: a.reshape(bsz, t, -1)
    grp = lambda a: a.reshape(bsz, t // CMP_STRIDE, CMP_STRIDE * KV_COLS)
    kc = _compress(grp(k_c), pe[0], w1[0], w2[0], True)
    vc = _compress(grp(v_c), pe[1], w1[1], w2[1], False)
    tk = _round_up(t, KEY_TILE)
    padk = lambda a: jnp.pad(b3(a), ((0, 0), (0, tk - t), (0, 0)))
    o = _nsa_attention(b3(q), kc, vc, padk(k_s), padk(v_s), b3(k_w), b3(v_w), b3(gates),
                       t_total=t, qt0=0, n_qt=t // Q_TILE, w_base=0)
    new_p = _proj_ln(o.reshape(bsz * t, Q_COLS), w_o16, yp.reshape(bsz * t, d), g, b, alpha).reshape(bsz, t, d)
    win = lambda a: jnp.concatenate([jnp.zeros((bsz, w_buf, KV_COLS), F32), b3(a)], axis=1)[:, -w_buf:]
    st_p = tuple(kv4(a, bsz, t) for a in (k_c, v_c, k_s, v_s)) + (
        kv4(win(k_w), bsz, w_buf), kv4(win(v_w), bsz, w_buf))

    n_s = bd * s
    tabs_s = tuple(jnp.tile(tt, (bd, 1)) for tt in _rope_tables(past + jnp.arange(s)))
    qs, k_c2, v_c2, k_s2, v_s2, k_w2, v_w2, gates_s = _nsa_proj(ys.reshape(n_s, d), w_pad, tabs_s, 1)
    s3 = lambda a: a.reshape(bd, s, -1)
    n_cmp_s = (past + s - CMP_LEN) // CMP_STRIDE + 1
    assert n_cmp_s <= past // CMP_STRIDE
    kc_s, vc_s = _compress_paged(pool_kc, pool_vc, page_table, pe, w1, w2, n_cmp_s)
    o_s = _attn_sample(qs, kc_s, vc_s, pool_ks, pool_vs, page_table, k_s2, v_s2, kw_buf, vw_buf, k_w2, v_w2,
                       gates_s, s_new=s)
    kw_all = jnp.concatenate([kw_buf.reshape(bd, w_buf, KV_COLS), s3(k_w2)], axis=1)
    vw_all = jnp.concatenate([vw_buf.reshape(bd, w_buf, KV_COLS), s3(v_w2)], axis=1)
    new_s = _proj_ln(o_s, w_o16, ys.reshape(n_s, d), g, b, alpha).reshape(bd, s, d)
    st_s = tuple(kv4(a, bd, s) for a in (k_c2, v_c2, k_s2, v_s2)) + (
        kv4(kw_all[:, -w_buf:], bd, w_buf), kv4(vw_all[:, -w_buf:], bd, w_buf))
    return new_p, new_s, st_p, st_s


def kernel(x_prompt, x_sample, state_conv, cache_k_cmp, cache_v_cmp, cache_k_sel, cache_v_sel, state_k_win, state_v_win, page_table, ln_g, ln_b, conv_w_pw1, conv_b_pw1, conv_w_dw, conv_b_dw, conv_ln_g, conv_ln_b, conv_w_pw2, conv_b_pw2, nsa_w_in, nsa_cmp_pe, nsa_cmp_w1, nsa_cmp_w2, nsa_w_o, ffn_w_gate, ffn_w_up, ffn_w_down, moe_w_router, moe_w_gate, moe_w_up, moe_w_down):
    depth = ln_g.shape[0]
    alpha = (2 * depth) ** 0.25
    bsz, t, d = x_prompt.shape
    bd, s, _ = x_sample.shape
    yp, ys = x_prompt, x_sample
    conv_p, conv_s, nsa_p, nsa_s = [], [], [], []
    for i in range(depth):
        j = i // 2
        if i % 2 == 0:
            yp, ys, st_p, st_s = _conv_layer(
                yp, ys, state_conv[j], conv_w_pw1[j], conv_b_pw1[j], conv_w_dw[j], conv_b_dw[j],
                conv_ln_g[j], conv_ln_b[j], conv_w_pw2[j], conv_b_pw2[j], ln_g[i, 0], ln_b[i, 0], alpha)
            conv_p.append(st_p)
            conv_s.append(st_s)
        else:
            yp, ys, st_p, st_s = _nsa_layer(
                yp, ys, cache_k_cmp[j], cache_v_cmp[j], cache_k_sel[j], cache_v_sel[j],
                state_k_win[j], state_v_win[j], page_table,
                nsa_w_in[j], nsa_cmp_pe[j], nsa_cmp_w1[j], nsa_cmp_w2[j], nsa_w_o[j],
                ln_g[i, 0], ln_b[i, 0], alpha)
            nsa_p.append(st_p)
            nsa_s.append(st_s)
        g2, b2 = ln_g[i, 1][None, :], ln_b[i, 1][None, :]
        xp, xs = yp.reshape(bsz * t, d), ys.reshape(bd * s, d)
        if i % 2 == 0:
            fw = (ffn_w_gate[j].astype(BF16), ffn_w_up[j].astype(BF16), ffn_w_down[j].astype(BF16))
            xp = _ffn_ln(xp, *fw, g2, b2, alpha)
            xs = _ffn_ln(xs, *fw, g2, b2, alpha)
        else:
            mw = (moe_w_gate[j].astype(BF16), moe_w_up[j].astype(BF16), moe_w_down[j].astype(BF16))
            xp = _moe_ln(xp, _router(xp, moe_w_router[j]), *mw, g2, b2, alpha)
            xs = _moe_ln(xs, _router(xs, moe_w_router[j]), *mw, g2, b2, alpha)
        yp, ys = xp.reshape(bsz, t, d), xs.reshape(bd, s, d)
    stack = lambda parts: tuple(jnp.stack(a) for a in zip(*parts))
    return ((yp, ys, jnp.stack(conv_p)) + stack(nsa_p) + (jnp.stack(conv_s),) + stack(nsa_s))
```

```python
import functools

import numpy as np
import jax
import jax.numpy as jnp
from jax import lax
from jax.experimental import pallas as pl
from jax.experimental.pallas import tpu as pltpu

F32 = jnp.float32
BF16 = jnp.bfloat16

N_HEADS = 16
N_KV_HEADS = 4
GROUP = N_HEADS // N_KV_HEADS
HEAD_DIM = 64
ROT_DIM = HEAD_DIM // 4
ROPE_THETA = 500000.0
CMP_LEN = 32
CMP_STRIDE = 16
SEL_BLOCK = 64
N_SELECT = 16
N_LOCAL_SEL = 2
WINDOW = 512
TOP_K = 2
LN_EPS = 1e-5
NEG = -1e30
BIG = 1e30

Q_TILE = 128
KEY_TILE = 512
KV_COLS = N_KV_HEADS * HEAD_DIM
Q_COLS = N_HEADS * HEAD_DIM
LANES = 128
VMEM_LIMIT = 56 * 1024 * 1024


def _cparams(*sem):
    return pltpu.CompilerParams(dimension_semantics=sem, vmem_limit_bytes=VMEM_LIMIT)


def _round_up(a, m):
    return -(-a // m) * m


def _row_tile(n, pref):
    t = min(n, pref)
    while n % t:
        t //= 2
    assert t >= 8 and n % t == 0
    return t


def _sigmoid(v):
    return 1.0 / (1.0 + jnp.exp(-v))


def _silu(v):
    return v * _sigmoid(v)


def _layer_norm(v, g, b):
    mu = jnp.mean(v, axis=-1, keepdims=True)
    d = v - mu
    var = jnp.mean(d * d, axis=-1, keepdims=True)
    return d * lax.rsqrt(var + LN_EPS) * g + b


def _dot(a, b):
    return jnp.dot(a, b, preferred_element_type=F32)


def _split_bf16(v):
    hi = v.astype(BF16)
    lo = (v - hi.astype(F32)).astype(BF16)
    return hi, lo


def _pw1_glu_kernel(x_ref, wa_ref, wg_ref, ba_ref, bg_ref, u_ref):
    x = x_ref[...].astype(BF16)
    a = _dot(x, wa_ref[...]) + ba_ref[...]
    g = _dot(x, wg_ref[...]) + bg_ref[...]
    u_ref[...] = a * _sigmoid(g)


def _pw1_glu(x, wa, wg, ba, bg):
    n, d = x.shape
    tm = _row_tile(n, 512)
    full = lambda i: (0, 0)
    return pl.pallas_call(
        _pw1_glu_kernel,
        grid=(n // tm,),
        in_specs=[pl.BlockSpec((tm, d), lambda i: (i, 0)),
                  pl.BlockSpec(wa.shape, full), pl.BlockSpec(wg.shape, full),
                  pl.BlockSpec(ba.shape, full), pl.BlockSpec(bg.shape, full)],
        out_specs=pl.BlockSpec((tm, d), lambda i: (i, 0)),
        out_shape=jax.ShapeDtypeStruct((n, d), F32),
        compiler_params=_cparams("parallel"),
    )(x, wa, wg, ba, bg)


HALO = 32


def _dw_taps(ext_ref, wdw_ref, rows, width):
    off = HALO - (width - 1)
    acc = wdw_ref[0:1, :] * ext_ref[pl.ds(off, rows), :]
    for k in range(1, width):
        acc = acc + wdw_ref[k:k + 1, :] * ext_ref[pl.ds(off + k, rows), :]
    return acc


def _conv2_prompt_kernel(width, alpha, u_ref, halo_ref, x_ref, wdw_ref, bdw_ref, cg_ref, cb_ref,
                         w2_ref, b2_ref, g_ref, b_ref, o_ref, ext_ref):
    i = pl.program_id(1)
    tm = u_ref.shape[1]

    @pl.when(i == 0)
    def _():
        ext_ref[0:HALO, :] = jnp.zeros((HALO, ext_ref.shape[1]), F32)

    @pl.when(i > 0)
    def _():
        ext_ref[0:HALO, :] = halo_ref[0]

    ext_ref[HALO:, :] = u_ref[0]
    c = _dw_taps(ext_ref, wdw_ref, tm, width) + bdw_ref[...]
    c = _layer_norm(c, cg_ref[...], cb_ref[...])
    h = _dot(_silu(c).astype(BF16), w2_ref[...]) + b2_ref[...]
    o_ref[0] = _layer_norm(alpha * x_ref[0] + h, g_ref[...], b_ref[...])


def _conv2_prompt(u, x, wdw, bdw, cg, cb, w2, b2, g, b, alpha):
    bsz, t, d = u.shape
    width = wdw.shape[0]
    tm = _row_tile(t, 256)
    assert tm % HALO == 0 and width - 1 <= HALO
    r = tm // HALO
    full = lambda bi, i: (0, 0)
    vec = pl.BlockSpec((1, d), full)
    return pl.pallas_call(
        functools.partial(_conv2_prompt_kernel, width, alpha),
        grid=(bsz, t // tm),
        in_specs=[pl.BlockSpec((1, tm, d), lambda bi, i: (bi, i, 0)),
                  pl.BlockSpec((1, HALO, d), lambda bi, i: (bi, jnp.maximum(i * r - 1, 0), 0)),
                  pl.BlockSpec((1, tm, d), lambda bi, i: (bi, i, 0)),
                  pl.BlockSpec(wdw.shape, full), vec, vec, vec,
                  pl.BlockSpec(w2.shape, full), vec, vec, vec],
        out_specs=pl.BlockSpec((1, tm, d), lambda bi, i: (bi, i, 0)),
        out_shape=jax.ShapeDtypeStruct((bsz, t, d), F32),
        scratch_shapes=[pltpu.VMEM((HALO + tm, d), F32)],
        compiler_params=_cparams("parallel", "arbitrary"),
    )(u, u, x, wdw, bdw, cg, cb, w2, b2, g, b)


def _conv2_sample_kernel(width, alpha, ext_ref, x_ref, wdw_ref, bdw_ref, cg_ref, cb_ref,
                         w2_ref, b2_ref, g_ref, b_ref, o_ref):
    bsz, rows, d = x_ref.shape
    acc = wdw_ref[0:1, :][None] * ext_ref[:, pl.ds(0, rows), :]
    for k in range(1, width):
        acc = acc + wdw_ref[k:k + 1, :][None] * ext_ref[:, pl.ds(k, rows), :]
    c = acc.reshape(bsz * rows, d) + bdw_ref[...]
    c = _layer_norm(c, cg_ref[...], cb_ref[...])
    h = _dot(_silu(c).astype(BF16), w2_ref[...]) + b2_ref[...]
    x = x_ref[...].reshape(bsz * rows, d)
    o_ref[...] = _layer_norm(alpha * x + h, g_ref[...], b_ref[...]).reshape(bsz, rows, d)


def _conv2_sample(ext, x, wdw, bdw, cg, cb, w2, b2, g, b, alpha):
    bsz, rows, d = x.shape
    return pl.pallas_call(
        functools.partial(_conv2_sample_kernel, wdw.shape[0], alpha),
        out_shape=jax.ShapeDtypeStruct((bsz, rows, d), F32),
        compiler_params=pltpu.CompilerParams(vmem_limit_bytes=VMEM_LIMIT),
    )(ext, x, wdw, bdw, cg, cb, w2, b2, g, b)


def _ffn_kernel(alpha, x_ref, wg_ref, wu_ref, wd_ref, g_ref, b_ref, o_ref, acc_ref):
    k = pl.program_id(1)

    @pl.when(k == 0)
    def _():
        acc_ref[...] = jnp.zeros(acc_ref.shape, F32)

    x = x_ref[...].astype(BF16)
    h = _silu(_dot(x, wg_ref[...])) * _dot(x, wu_ref[...])
    acc_ref[...] += _dot(h.astype(BF16), wd_ref[...])

    @pl.when(k == pl.num_programs(1) - 1)
    def _():
        o_ref[...] = _layer_norm(alpha * x_ref[...] + acc_ref[...], g_ref[...], b_ref[...])


def _ff_tile(dff):
    for tf in (512, 384, 256, 128):
        if dff % tf == 0:
            return tf
    return dff


def _ffn_ln(x, wg, wu, wd, g, b, alpha):
    n, d = x.shape
    dff = wg.shape[1]
    tm = _row_tile(n, 1024)
    tf = _ff_tile(dff)
    vec = pl.BlockSpec((1, d), lambda i, k: (0, 0))
    return pl.pallas_call(
        functools.partial(_ffn_kernel, alpha),
        grid=(n // tm, dff // tf),
        in_specs=[pl.BlockSpec((tm, d), lambda i, k: (i, 0)),
                  pl.BlockSpec((d, tf), lambda i, k: (0, k)),
                  pl.BlockSpec((d, tf), lambda i, k: (0, k)),
                  pl.BlockSpec((tf, d), lambda i, k: (k, 0)), vec, vec],
        out_specs=pl.BlockSpec((tm, d), lambda i, k: (i, 0)),
        out_shape=jax.ShapeDtypeStruct((n, d), F32),
        scratch_shapes=[pltpu.VMEM((tm, d), F32)],
        compiler_params=_cparams("parallel", "arbitrary"),
    )(x, wg, wu, wd, g, b)


def _router_kernel(n_exp, x_ref, wh_ref, wl_ref, comb_ref):
    xh, xl = _split_bf16(x_ref[...])
    logits = _dot(xh, wh_ref[...]) + (_dot(xh, wl_ref[...]) + _dot(xl, wh_ref[...]))
    lane = lax.broadcasted_iota(jnp.int32, logits.shape, 1)
    logits = jnp.where(lane < n_exp, logits, -jnp.inf)
    m1 = jnp.max(logits, axis=-1, keepdims=True)
    i1 = jnp.min(jnp.where(logits == m1, lane, LANES), axis=-1, keepdims=True)
    rest = jnp.where(lane == i1, -jnp.inf, logits)
    m2 = jnp.max(rest, axis=-1, keepdims=True)
    i2 = jnp.min(jnp.where(rest == m2, lane, LANES), axis=-1, keepdims=True)
    e2 = jnp.exp(m2 - m1)
    w1 = 1.0 / (1.0 + e2)
    w2 = e2 / (1.0 + e2)
    comb_ref[...] = jnp.where(lane == i1, w1, 0.0) + jnp.where(lane == i2, w2, 0.0)


def _router(x, w_router):
    n, d = x.shape
    n_exp = w_router.shape[1]
    assert TOP_K == 2 and n_exp <= LANES
    wpad = jnp.pad(w_router, ((0, 0), (0, LANES - n_exp)))
    wh = wpad.astype(BF16)
    wl = (wpad - wh.astype(F32)).astype(BF16)
    tm = _row_tile(n, 512)
    return pl.pallas_call(
        functools.partial(_router_kernel, n_exp),
        grid=(n // tm,),
        in_specs=[pl.BlockSpec((tm, d), lambda i: (i, 0)),
                  pl.BlockSpec((d, LANES), lambda i: (0, 0)),
                  pl.BlockSpec((d, LANES), lambda i: (0, 0))],
        out_specs=pl.BlockSpec((tm, LANES), lambda i: (i, 0)),
        out_shape=jax.ShapeDtypeStruct((n, LANES), F32),
        compiler_params=_cparams("parallel"),
    )(x, wh, wl)


def _moe_kernel(alpha, x_ref, comb_ref, wg_ref, wu_ref, wd_ref, g_ref, b_ref, o_ref, acc_ref):
    e = pl.program_id(1)
    k = pl.program_id(2)

    @pl.when((e == 0) & (k == 0))
    def _():
        acc_ref[...] = jnp.zeros(acc_ref.shape, F32)

    comb = comb_ref[...]
    lane = lax.broadcasted_iota(jnp.int32, comb.shape, 1)
    c = jnp.sum(jnp.where(lane == e, comb, 0.0), axis=-1, keepdims=True)
    x = x_ref[...].astype(BF16)
    h = _silu(_dot(x, wg_ref[0])) * _dot(x, wu_ref[0])
    acc_ref[...] += c * _dot(h.astype(BF16), wd_ref[0])

    @pl.when((e == pl.num_programs(1) - 1) & (k == pl.num_programs(2) - 1))
    def _():
        o_ref[...] = _layer_norm(alpha * x_ref[...] + acc_ref[...], g_ref[...], b_ref[...])


def _moe_ln(x, comb, wg, wu, wd, g, b, alpha):
    n, d = x.shape
    n_exp, _, dff = wg.shape
    tm = _row_tile(n, 1024)
    tf = _ff_tile(dff)
    vec = pl.BlockSpec((1, d), lambda i, e, k: (0, 0))
    return pl.pallas_call(
        functools.partial(_moe_kernel, alpha),
        grid=(n // tm, n_exp, dff // tf),
        in_specs=[pl.BlockSpec((tm, d), lambda i, e, k: (i, 0)),
                  pl.BlockSpec((tm, LANES), lambda i, e, k: (i, 0)),
                  pl.BlockSpec((1, d, tf), lambda i, e, k: (e, 0, k)),
                  pl.BlockSpec((1, d, tf), lambda i, e, k: (e, 0, k)),
                  pl.BlockSpec((1, tf, d), lambda i, e, k: (e, k, 0)), vec, vec],
        out_specs=pl.BlockSpec((tm, d), lambda i, e, k: (i, 0)),
        out_shape=jax.ShapeDtypeStruct((n, d), F32),
        scratch_shapes=[pltpu.VMEM((tm, d), F32)],
        compiler_params=_cparams("parallel", "arbitrary", "arbitrary"),
    )(x, comb, wg, wu, wd, g, b)


def _rope_chunk(v, cos, sa, sb):
    half = ROT_DIM // 2
    return v * cos + pltpu.roll(v, LANES - half, 1) * sa + pltpu.roll(v, half, 1) * sb


def _nsa_proj_kernel(x_ref, w_ref, cos_ref, sa_ref, sb_ref,
                     q_ref, kc_ref, vc_ref, ks_ref, vs_ref, kw_ref, vw_ref, gate_ref):
    y = _dot(x_ref[...].astype(BF16), w_ref[...])
    cos, sa, sb = cos_ref[...], sa_ref[...], sb_ref[...]
    scale = HEAD_DIM ** -0.5
    for j in range(Q_COLS // LANES):
        q_ref[:, j * LANES:(j + 1) * LANES] = (
            _rope_chunk(y[:, j * LANES:(j + 1) * LANES], cos, sa, sb) * scale).astype(BF16)
    base = Q_COLS
    for idx, (ref, roped) in enumerate(((kc_ref, False), (vc_ref, False), (ks_ref, True),
                                        (vs_ref, False), (kw_ref, True), (vw_ref, False))):
        for j in range(KV_COLS // LANES):
            lo = base + idx * KV_COLS + j * LANES
            v = y[:, lo:lo + LANES]
            ref[:, j * LANES:(j + 1) * LANES] = _rope_chunk(v, cos, sa, sb) if roped else v
    gate_ref[...] = _sigmoid(y[:, base + 6 * KV_COLS:base + 6 * KV_COLS + LANES])


def _rope_tables(pos):
    half = ROT_DIM // 2
    inv_freq = ROPE_THETA ** (-jnp.arange(half, dtype=F32) / half)
    ang = pos.astype(F32)[:, None] * inv_freq[None, :]
    cos, sin = jnp.cos(ang), jnp.sin(ang)
    n = pos.shape[0]
    rest = HEAD_DIM - ROT_DIM
    one = jnp.ones((n, rest), F32)
    zero = jnp.zeros((n, rest), F32)
    zh = jnp.zeros((n, half), F32)
    c = jnp.concatenate([cos, cos, one], axis=1)
    sa = jnp.concatenate([-sin, zh, zero], axis=1)
    sb = jnp.concatenate([zh, sin, zero], axis=1)
    rep = LANES // HEAD_DIM
    return tuple(jnp.tile(t, (1, rep)) for t in (c, sa, sb))


def _nsa_proj(x, w_pad, tables, n_tab_blocks):
    n, d = x.shape
    tm = tables[0].shape[0] // n_tab_blocks
    assert n % tm == 0
    row = lambda i: (i, 0)
    tab = pl.BlockSpec((tm, LANES), lambda i: (i % n_tab_blocks, 0))
    kv_spec = pl.BlockSpec((tm, KV_COLS), row)
    kv_shape = jax.ShapeDtypeStruct((n, KV_COLS), F32)
    return pl.pallas_call(
        _nsa_proj_kernel,
        grid=(n // tm,),
        in_specs=[pl.BlockSpec((tm, d), row), pl.BlockSpec(w_pad.shape, lambda i: (0, 0)), tab, tab, tab],
        out_specs=[pl.BlockSpec((tm, Q_COLS), row)] + [kv_spec] * 6 + [pl.BlockSpec((tm, LANES), row)],
        out_shape=[jax.ShapeDtypeStruct((n, Q_COLS), BF16)] + [kv_shape] * 6
                  + [jax.ShapeDtypeStruct((n, LANES), F32)],
        compiler_params=_cparams("parallel"),
    )(x, w_pad, *tables)


def _compress_kernel(roped, g_ref, pea_ref, peb_ref, wa_ref, wb_ref, w2_ref, cos_ref, sa_ref, sb_ref, o_ref):
    grp = g_ref[0]
    rows = grp.shape[0]
    a = _dot((grp + pea_ref[...]).astype(BF16), wa_ref[...])
    bm = _dot((grp + peb_ref[...]).astype(BF16), wb_ref[...])
    h = a + pltpu.roll(bm, rows - 1, 0)
    out = _dot(_silu(h).astype(BF16), w2_ref[...])
    if roped:
        cos, sa, sb = cos_ref[...], sa_ref[...], sb_ref[...]
        for j in range(KV_COLS // LANES):
            o_ref[0, :, j * LANES:(j + 1) * LANES] = _rope_chunk(out[:, j * LANES:(j + 1) * LANES], cos, sa, sb)
    else:
        o_ref[0] = out


def _compress(groups, pe, w1, w2, roped):
    bsz, ng, gw = groups.shape
    assert CMP_LEN == 2 * CMP_STRIDE and gw == CMP_STRIDE * KV_COLS
    pea, peb, wa, wb, w2b = _compress_weights(pe, w1, w2)
    c_end = jnp.arange(ng) * CMP_STRIDE + (CMP_LEN - 1)
    tables = _rope_tables(c_end)
    full = lambda b: (0, 0)
    tab = pl.BlockSpec((ng, LANES), full)
    return pl.pallas_call(
        functools.partial(_compress_kernel, roped),
        grid=(bsz,),
        in_specs=[pl.BlockSpec((1, ng, gw), lambda b: (b, 0, 0)),
                  pl.BlockSpec((1, gw), full), pl.BlockSpec((1, gw), full),
                  pl.BlockSpec(wa.shape, full), pl.BlockSpec(wb.shape, full), pl.BlockSpec(w2b.shape, full),
                  tab, tab, tab],
        out_specs=pl.BlockSpec((1, ng, KV_COLS), lambda b: (b, 0, 0)),
        out_shape=jax.ShapeDtypeStruct((bsz, ng, KV_COLS), F32),
        compiler_params=_cparams("parallel"),
    )(groups, pea, peb, wa, wb, w2b, *tables)


def _softmax_cols(s, mask, col_valid=None):
    s = jnp.where(mask, s, NEG)
    p = jnp.exp(s - jnp.max(s, axis=0, keepdims=True))
    r = 1.0 / jnp.maximum(jnp.sum(p, axis=0, keepdims=True), 1e-30)
    return p * (r if col_valid is None else jnp.where(col_valid, r, 0.0))


def _attn_kernel(n_cmp, n_blk, qT_ref, kc_ref, vcT_ref, oh_ref, ks_ref, vsT_ref, kw_ref, vwT_ref,
                 gate_ref, ovT_ref, o_ref):
    q0 = pl.program_id(2) * Q_TILE
    qT = qT_ref[0, 0, 0]
    cols = qT.shape[1]
    qpos = q0 + lax.broadcasted_iota(jnp.int32, (1, cols), 1) % Q_TILE
    nb_pad = ovT_ref.shape[0]

    kc = kc_ref[0, 0]
    ncp = kc.shape[0]
    cidx = lax.broadcasted_iota(jnp.int32, (ncp, 1), 0)
    c_end = jnp.where(cidx < n_cmp, cidx * CMP_STRIDE + (CMP_LEN - 1), jnp.iinfo(jnp.int32).max)
    p_c = _softmax_cols(_dot(kc, qT), c_end <= qpos, qpos >= CMP_LEN - 1)
    o_c = _dot(vcT_ref[0, 0], p_c.astype(BF16))

    p_sum = p_c[:, 0:Q_TILE]
    for g in range(1, GROUP):
        p_sum = p_sum + p_c[:, g * Q_TILE:(g + 1) * Q_TILE]
    p_hi, p_lo = _split_bf16(p_sum)
    imp = _dot(ovT_ref[...], p_hi) + _dot(ovT_ref[...], p_lo)

    blk = lax.broadcasted_iota(jnp.int32, (nb_pad, 1), 0)
    qp = qpos[:, 0:Q_TILE]
    lag = qp // SEL_BLOCK - blk
    forced = (blk == 0) | ((lag >= 0) & (lag < N_LOCAL_SEL))
    score = jnp.where(forced, BIG, jnp.where(blk * SEL_BLOCK <= qp, imp, -BIG))
    score = jnp.where(blk < n_blk, score, -jnp.inf)
    bias = jnp.full(score.shape, NEG, F32)
    for _ in range(min(N_SELECT, n_blk)):
        m = jnp.max(score, axis=0, keepdims=True)
        j = jnp.min(jnp.where(score == m, blk, nb_pad), axis=0, keepdims=True)
        hit = blk == j
        bias = jnp.where(hit, 0.0, bias)
        score = jnp.where(hit, -jnp.inf, score)
    bias = bias.astype(BF16)
    q_aug = jnp.concatenate([jnp.concatenate([bias] * GROUP, axis=1), qT], axis=0)

    def scores(kb):
        k0 = pl.multiple_of(kb * KEY_TILE, KEY_TILE)
        keys = jnp.concatenate([oh_ref[pl.ds(k0, KEY_TILE), :], ks_ref[0, 0, pl.ds(k0, KEY_TILE), :]], axis=1)
        kpos = k0 + lax.broadcasted_iota(jnp.int32, (KEY_TILE, 1), 0)
        return jnp.where(kpos <= qpos, _dot(keys, q_aug), NEG)

    def accumulate(kb, s, m, l, acc):
        k0 = pl.multiple_of(kb * KEY_TILE, KEY_TILE)
        m_new = jnp.maximum(m, jnp.max(s, axis=0, keepdims=True))
        alpha = jnp.exp(m - m_new)
        p = jnp.exp(s - m_new)
        l = alpha * l + jnp.sum(p, axis=0, keepdims=True)
        acc = alpha * acc + _dot(vsT_ref[0, 0, :, pl.ds(k0, KEY_TILE)], p.astype(BF16))
        return m_new, l, acc

    def body(kb, carry):
        s, m, l, acc = carry
        return (scores(kb + 1),) + accumulate(kb, s, m, l, acc)

    n_full = q0 // KEY_TILE
    init = (scores(0), jnp.full((1, cols), NEG, F32), jnp.zeros((1, cols), F32),
            jnp.zeros((HEAD_DIM, cols), F32))
    s_last, m_s, l_s, acc_s = lax.fori_loop(0, n_full, body, init)
    _, l_s, acc_s = accumulate(n_full, s_last, m_s, l_s, acc_s)
    o_s = acc_s * (1.0 / jnp.maximum(l_s, 1e-30))

    wlen = WINDOW + Q_TILE
    w0 = pl.multiple_of(jnp.maximum(q0 - WINDOW, 0), Q_TILE)
    wpos = w0 + lax.broadcasted_iota(jnp.int32, (wlen, 1), 0)
    in_window = jnp.abs((2 * qpos - (WINDOW - 1)) - 2 * wpos) <= WINDOW - 1
    p_w = _softmax_cols(_dot(kw_ref[0, 0, pl.ds(w0, wlen), :], qT), in_window)
    o_w = _dot(vwT_ref[0, 0, :, pl.ds(w0, wlen)], p_w.astype(BF16))

    gate = gate_ref[0, 0, 0]
    o_ref[0, 0, 0] = (gate[0:1] * o_c + gate[1:2] * o_s + gate[2:3] * o_w).astype(BF16)


def _overlap_t(n_cmp, n_cmp_pad, n_blk, nb_pad):
    cs = np.arange(n_cmp_pad)[None, :] * CMP_STRIDE
    ss = np.arange(nb_pad)[:, None] * SEL_BLOCK
    ov = np.maximum(np.minimum(cs + CMP_LEN, ss + SEL_BLOCK) - np.maximum(cs, ss), 0) / CMP_LEN
    ov = ov * (np.arange(n_cmp_pad)[None, :] < n_cmp) * (np.arange(nb_pad)[:, None] < n_blk)
    return jnp.asarray(ov, BF16)


def _nsa_attention(q, kc, vc, ks, vs, kw, vw, gates):
    bsz, t, _ = q.shape
    n_qt = t // Q_TILE
    n_cmp = (t - CMP_LEN) // CMP_STRIDE + 1
    n_blk = -(-t // SEL_BLOCK)
    nb_pad = _round_up(n_blk, 16)
    ncp = kc.shape[1]
    cols = GROUP * Q_TILE
    assert t % KEY_TILE == 0 and t >= WINDOW + Q_TILE and n_cmp <= ncp

    heads = lambda a: a.astype(BF16).reshape(bsz, a.shape[1], N_KV_HEADS, HEAD_DIM).transpose(0, 2, 1, 3)
    heads_t = lambda a: a.astype(BF16).reshape(bsz, a.shape[1], N_KV_HEADS, HEAD_DIM).transpose(0, 2, 3, 1)
    qT = q.reshape(bsz, n_qt, Q_TILE, N_KV_HEADS, GROUP, HEAD_DIM).transpose(0, 3, 1, 5, 4, 2)
    qT = qT.reshape(bsz, N_KV_HEADS, n_qt, HEAD_DIM, cols)
    onehot = jnp.asarray(np.arange(t)[:, None] // SEL_BLOCK == np.arange(nb_pad)[None, :], BF16)
    gT = gates[..., :3 * N_HEADS].reshape(bsz, n_qt, Q_TILE, N_KV_HEADS, GROUP, 3).transpose(0, 3, 1, 5, 4, 2)
    gT = jnp.pad(gT.reshape(bsz, N_KV_HEADS, n_qt, 3, cols), ((0, 0), (0, 0), (0, 0), (0, 5), (0, 0)))
    ovT = _overlap_t(n_cmp, ncp, n_blk, nb_pad)

    per_head = lambda *shape: pl.BlockSpec((1, 1) + shape, lambda b, h, i: (b, h, 0, 0))
    per_tile = lambda *shape: pl.BlockSpec((1, 1, 1) + shape, lambda b, h, i: (b, h, i, 0, 0))
    const = lambda a: pl.BlockSpec(a.shape, lambda b, h, i: (0, 0))
    oT = pl.pallas_call(
        functools.partial(_attn_kernel, n_cmp, n_blk),
        grid=(bsz, N_KV_HEADS, n_qt),
        in_specs=[per_tile(HEAD_DIM, cols),
                  per_head(ncp, HEAD_DIM), per_head(HEAD_DIM, ncp),
                  const(onehot), per_head(t, HEAD_DIM), per_head(HEAD_DIM, t),
                  per_head(t, HEAD_DIM), per_head(HEAD_DIM, t),
                  per_tile(8, cols), const(ovT)],
        out_specs=per_tile(HEAD_DIM, cols),
        out_shape=jax.ShapeDtypeStruct((bsz, N_KV_HEADS, n_qt, HEAD_DIM, cols), BF16),
        compiler_params=_cparams("parallel", "parallel", "arbitrary"),
    )(qT, heads(kc), heads_t(vc), onehot, heads(ks), heads_t(vs), heads(kw), heads_t(vw), gT, ovT)
    o = oT.reshape(bsz, N_KV_HEADS, n_qt, HEAD_DIM, GROUP, Q_TILE).transpose(0, 2, 5, 1, 4, 3)
    return o.reshape(bsz, t, Q_COLS)


def _paged_pipeline(pt_ref, pools, bufs, sems, pages_per_chunk, extra_pages, rows_per_page):
    seq, chunk = pl.program_id(0), pl.program_id(1)
    n_chunk = pl.num_programs(1)
    step = seq * n_chunk + chunk
    total = pl.num_programs(0) * n_chunk
    slot = step % 2
    last_page = n_chunk * pages_per_chunk - 1
    n_copy = pages_per_chunk + extra_pages

    def start(sq, ck, sl):
        def body(j, carry):
            page = pt_ref[sq, jnp.minimum(ck * pages_per_chunk + j, last_page)]
            dst = pl.ds(pl.multiple_of(j * rows_per_page, rows_per_page), rows_per_page)
            for i, (pool, buf) in enumerate(zip(pools, bufs)):
                pltpu.make_async_copy(pool.at[page], buf.at[sl, dst], sems.at[i, sl]).start()
            return carry
        lax.fori_loop(0, n_copy, body, 0)

    @pl.when(step == 0)
    def _():
        start(seq, chunk, slot)

    @pl.when(step + 1 < total)
    def _():
        nxt = step + 1
        start(nxt // n_chunk, nxt % n_chunk, 1 - slot)

    def wait_body(j, carry):
        dst = pl.ds(pl.multiple_of(j * rows_per_page, rows_per_page), rows_per_page)
        for i, (pool, buf) in enumerate(zip(pools, bufs)):
            pltpu.make_async_copy(pool.at[0], buf.at[slot, dst], sems.at[i, slot]).wait()
        return carry
    lax.fori_loop(0, n_copy, wait_body, 0)
    return slot


def _compress_paged_kernel(pages_per_chunk, n_cmp, pt_ref, poolk_ref, poolv_ref,
                           peak_ref, pebk_ref, wak_ref, wbk_ref, w2k_ref,
                           peav_ref, pebv_ref, wav_ref, wbv_ref, w2v_ref,
                           cos_ref, sa_ref, sb_ref, ok_ref, ov_ref, kbuf, vbuf, sems):
    gpp = poolk_ref.shape[1]
    rows = pages_per_chunk * gpp
    slot = _paged_pipeline(pt_ref, (poolk_ref, poolv_ref), (kbuf, vbuf), sems, pages_per_chunk, 1, gpp)
    row0 = pl.program_id(1) * rows
    valid = row0 + lax.broadcasted_iota(jnp.int32, (rows, 1), 0) < n_cmp

    def mlp(buf, pea_ref, peb_ref, wa_ref, wb_ref, w2_ref):
        grp = buf[slot]
        a = _dot((grp[:rows] + pea_ref[...]).astype(BF16), wa_ref[...])
        bm = _dot((grp + peb_ref[...]).astype(BF16), wb_ref[...])
        h = a + pltpu.roll(bm, rows + gpp - 1, 0)[:rows]
        return jnp.where(valid, _dot(_silu(h).astype(BF16), w2_ref[...]), 0.0)

    outk = mlp(kbuf, peak_ref, pebk_ref, wak_ref, wbk_ref, w2k_ref)
    cos, sa, sb = cos_ref[...], sa_ref[...], sb_ref[...]
    for j in range(KV_COLS // LANES):
        ok_ref[0, :, j * LANES:(j + 1) * LANES] = _rope_chunk(
            outk[:, j * LANES:(j + 1) * LANES], cos, sa, sb).astype(BF16)
    ov_ref[0] = mlp(vbuf, peav_ref, pebv_ref, wav_ref, wbv_ref, w2v_ref).astype(BF16)


def _compress_weights(pe, w1, w2):
    eye = jnp.eye(N_KV_HEADS, dtype=F32)
    blockdiag = lambda w: jnp.einsum('lde,hg->lhdge', w, eye).reshape(-1, KV_COLS).astype(BF16)
    w2b = jnp.einsum('de,hg->hdge', w2, eye).reshape(KV_COLS, KV_COLS).astype(BF16)
    pe_row = lambda p: jnp.tile(p[:, None, :], (1, N_KV_HEADS, 1)).reshape(1, -1)
    return (pe_row(pe[:CMP_STRIDE]), pe_row(pe[CMP_STRIDE:]),
            blockdiag(w1[:CMP_STRIDE]), blockdiag(w1[CMP_STRIDE:]), w2b)


def _compress_paged(pool_k, pool_v, page_table, pe, w1, w2, n_cmp):
    bd, n_pages = page_table.shape
    page = pool_k.shape[1]
    gpp = page // CMP_STRIDE
    gw = CMP_STRIDE * KV_COLS
    n_chunk = 2 if n_pages % 2 == 0 else 1
    ppc = n_pages // n_chunk
    rows = ppc * gpp
    ng = n_pages * gpp
    as_groups = lambda pool: pool.reshape(pool.shape[0], gpp, gw)
    wk = _compress_weights(pe[0], w1[0], w2[0])
    wv = _compress_weights(pe[1], w1[1], w2[1])
    tables = _rope_tables(jnp.arange(ng) * CMP_STRIDE + (CMP_LEN - 1))
    full = lambda b, c, pt: (0, 0)
    wspecs = [pl.BlockSpec(w.shape, full) for w in wk + wv]
    tab = pl.BlockSpec((rows, LANES), lambda b, c, pt: (c, 0))
    out_spec = pl.BlockSpec((1, rows, KV_COLS), lambda b, c, pt: (b, c, 0))
    out_shape = jax.ShapeDtypeStruct((bd, ng, KV_COLS), BF16)
    return pl.pallas_call(
        functools.partial(_compress_paged_kernel, ppc, n_cmp),
        grid_spec=pltpu.PrefetchScalarGridSpec(
            num_scalar_prefetch=1, grid=(bd, n_chunk),
            in_specs=[pl.BlockSpec(memory_space=pl.ANY), pl.BlockSpec(memory_space=pl.ANY)] + wspecs + [tab] * 3,
            out_specs=[out_spec, out_spec],
            scratch_shapes=[pltpu.VMEM((2, rows + gpp, gw), F32), pltpu.VMEM((2, rows + gpp, gw), F32),
                            pltpu.SemaphoreType.DMA((2, 2))]),
        out_shape=[out_shape, out_shape],
        compiler_params=_cparams("arbitrary", "arbitrary"),
    )(page_table, as_groups(pool_k), as_groups(pool_v), *wk, *wv, *tables)


NEW_PAD = 128
CHUNK_PAGES = 16


def _dot_nt(a, b):
    return lax.dot_general(a, b, (((1,), (1,)), ((), ())), preferred_element_type=F32)


def _softmax_rows2(s1, mask1, s2, mask2):
    s1, s2 = jnp.where(mask1, s1, NEG), jnp.where(mask2, s2, NEG)
    m = jnp.maximum(jnp.max(s1, axis=-1, keepdims=True), jnp.max(s2, axis=-1, keepdims=True))
    p1 = jnp.where(mask1, jnp.exp(s1 - m), 0.0)
    p2 = jnp.where(mask2, jnp.exp(s2 - m), 0.0)
    r = 1.0 / jnp.maximum(jnp.sum(p1, axis=-1, keepdims=True) + jnp.sum(p2, axis=-1, keepdims=True), 1e-30)
    return p1 * r, p2 * r


def _attn_sample_kernel(s_new, past, n_cmp, n_blk, pt_ref, poolk_ref, poolv_ref, q_ref, kc_ref, vc_ref,
                        knew_ref, vnew_ref, kwb_ref, vwb_ref, kwn_ref, vwn_ref, gate_ref,
                        et_ref, ett_ref, ov_ref, gsum_ref, o_ref,
                        kbuf, vbuf, sems, m_ref, l_ref, acc_ref, bias_ref, part_ref):
    page = poolk_ref.shape[1]
    slot = _paged_pipeline(pt_ref, (poolk_ref, poolv_ref), (kbuf, vbuf), sems, CHUNK_PAGES, 0, page)
    chunk = pl.program_id(1)
    q = q_ref[0]
    n_rows = q.shape[0]
    qpos = past + lax.broadcasted_iota(jnp.int32, (n_rows, 1), 0) % s_new
    gate = gate_ref[0]

    @pl.when(chunk == 0)
    def _():
        ncp = kc_ref.shape[1]
        cidx = lax.broadcasted_iota(jnp.int32, (1, ncp), 1)
        cmask = (cidx * CMP_STRIDE + (CMP_LEN - 1) <= qpos) & (cidx < n_cmp)
        s_c = jnp.where(cmask, _dot_nt(q, kc_ref[0]), NEG)
        p_c = jnp.where(cmask, jnp.exp(s_c - jnp.max(s_c, axis=-1, keepdims=True)), 0.0)
        p_c = p_c * (1.0 / jnp.maximum(jnp.sum(p_c, axis=-1, keepdims=True), 1e-30))
        o_c = _dot(p_c.astype(BF16), vc_ref[0])
        p_hi, p_lo = _split_bf16(p_c)
        p_sum = _dot(gsum_ref[...], p_hi) + _dot(gsum_ref[...], p_lo)
        s_hi, s_lo = _split_bf16(p_sum)
        imp = _dot(s_hi, ov_ref[...]) + _dot(s_lo, ov_ref[...])
        nb_pad = imp.shape[1]
        blk = lax.broadcasted_iota(jnp.int32, (1, nb_pad), 1)
        lag = qpos // SEL_BLOCK - blk
        forced = (blk == 0) | ((lag >= 0) & (lag < N_LOCAL_SEL))
        score = jnp.where(forced, BIG, jnp.where(blk * SEL_BLOCK <= qpos, imp, -BIG))
        score = jnp.where(blk < n_blk, score, -jnp.inf)
        bias = jnp.full(score.shape, NEG, F32)
        for _ in range(min(N_SELECT, n_blk)):
            m = jnp.max(score, axis=-1, keepdims=True)
            j = jnp.min(jnp.where(score == m, blk, nb_pad), axis=-1, keepdims=True)
            hit = blk == j
            bias = jnp.where(hit, 0.0, bias)
            score = jnp.where(hit, -jnp.inf, score)
        bias_ref[...] = bias
        w_buf = kwb_ref.shape[1]
        wpos = past - w_buf + lax.broadcasted_iota(jnp.int32, (1, w_buf), 1)
        npos = lax.broadcasted_iota(jnp.int32, (1, NEW_PAD), 1)
        p1, p2 = _softmax_rows2(
            _dot_nt(q, kwb_ref[0].astype(BF16)), (wpos <= qpos) & (wpos > qpos - WINDOW),
            _dot_nt(q, kwn_ref[0]), (past + npos <= qpos) & (past + npos > qpos - WINDOW) & (npos < s_new))
        o_w = _dot(p1.astype(BF16), vwb_ref[0].astype(BF16)) + _dot(p2.astype(BF16), vwn_ref[0])
        part_ref[...] = gate[:, 0:1] * o_c + gate[:, 2:3] * o_w
        m_ref[...] = jnp.full(m_ref.shape, NEG, F32)
        l_ref[...] = jnp.zeros(l_ref.shape, F32)
        acc_ref[...] = jnp.zeros(acc_ref.shape, F32)

    bias16 = bias_ref[...].astype(BF16)

    def flash(s, v):
        m_new = jnp.maximum(m_ref[...], jnp.max(s, axis=-1, keepdims=True))
        alpha = jnp.exp(m_ref[...] - m_new)
        p = jnp.exp(s - m_new)
        l_ref[...] = alpha * l_ref[...] + jnp.sum(p, axis=-1, keepdims=True)
        acc_ref[...] = alpha * acc_ref[...] + _dot(p.astype(BF16), v)
        m_ref[...] = m_new

    flash(_dot_nt(q, kbuf[slot].astype(BF16)) + _dot(bias16, et_ref[...]), vbuf[slot].astype(BF16))

    @pl.when(chunk == pl.num_programs(1) - 1)
    def _():
        npos = lax.broadcasted_iota(jnp.int32, (1, NEW_PAD), 1)
        s_t = _dot_nt(q, knew_ref[0]) + _dot(bias16, ett_ref[...])
        flash(jnp.where((past + npos <= qpos) & (npos < s_new), s_t, NEG), vnew_ref[0])
        o_s = acc_ref[...] * (1.0 / jnp.maximum(l_ref[...], 1e-30))
        o_ref[0] = part_ref[...] + gate[:, 1:2] * o_s


def _attn_sample(qs, kc, vc, pool_k, pool_v, page_table, k_new, v_new, kw_buf, vw_buf, kw_new, vw_new, gates,
                 *, s_new):
    bd, n_pages = page_table.shape
    page = pool_k.shape[1]
    past = n_pages * page
    w_buf = kw_buf.shape[1]
    n_rows = N_HEADS * s_new
    n_cmp = (past + s_new - CMP_LEN) // CMP_STRIDE + 1
    n_blk = -(-(past + s_new) // SEL_BLOCK)
    nb_pad = _round_up(n_blk, 2 * LANES)
    ncp = kc.shape[1]
    assert n_pages % CHUNK_PAGES == 0 and s_new <= NEW_PAD and n_cmp <= ncp and w_buf == min(WINDOW, past)
    ck = CHUNK_PAGES * page
    eye = jnp.eye(N_KV_HEADS, dtype=BF16)

    q5 = qs.reshape(bd, s_new, N_KV_HEADS, GROUP, HEAD_DIM).transpose(0, 2, 3, 1, 4)
    qbig = jnp.einsum('bkgsd,kj->bkgsjd', q5, eye).reshape(bd, n_rows, KV_COLS)
    g5 = gates[:, :3 * N_HEADS].reshape(bd, s_new, N_KV_HEADS, GROUP, 3).transpose(0, 2, 3, 1, 4)
    gate = jnp.pad(g5.reshape(bd, n_rows, 3), ((0, 0), (0, 0), (0, LANES - 3)))
    newrows = lambda a: jnp.pad(a.reshape(bd, s_new, KV_COLS), ((0, 0), (0, NEW_PAD - s_new), (0, 0))).astype(BF16)
    flat = lambda a: a.reshape(a.shape[0], a.shape[1], KV_COLS)

    keyblk = np.arange(past + NEW_PAD)[None, :] // SEL_BLOCK
    et = jnp.asarray(keyblk == np.arange(nb_pad)[:, None], BF16)
    cs = np.arange(ncp)[:, None] * CMP_STRIDE
    ss = np.arange(nb_pad)[None, :] * SEL_BLOCK
    ov = np.maximum(np.minimum(cs + CMP_LEN, ss + SEL_BLOCK) - np.maximum(cs, ss), 0) / CMP_LEN
    ov = jnp.asarray(ov * (np.arange(ncp)[:, None] < n_cmp) * (np.arange(nb_pad)[None, :] < n_blk), BF16)
    r = np.arange(n_rows)
    same = (r[:, None] // (GROUP * s_new) == r[None, :] // (GROUP * s_new)) & (r[:, None] % s_new == r[None, :] % s_new)
    gsum = jnp.asarray(same, BF16)

    per_seq = lambda a: pl.BlockSpec((1,) + a.shape[1:], lambda b, c, pt: (b, 0, 0))
    const = lambda a: pl.BlockSpec(a.shape, lambda b, c, pt: (0, 0))
    operands = [qbig, kc, vc, newrows(k_new), newrows(v_new), flat(kw_buf), flat(vw_buf),
                newrows(kw_new), newrows(vw_new), gate]
    o_big = pl.pallas_call(
        functools.partial(_attn_sample_kernel, s_new, past, n_cmp, n_blk),
        grid_spec=pltpu.PrefetchScalarGridSpec(
            num_scalar_prefetch=1, grid=(bd, n_pages // CHUNK_PAGES),
            in_specs=[pl.BlockSpec(memory_space=pl.ANY), pl.BlockSpec(memory_space=pl.ANY)]
                     + [per_seq(a) for a in operands]
                     + [pl.BlockSpec((nb_pad, ck), lambda b, c, pt: (0, c)),
                        pl.BlockSpec((nb_pad, NEW_PAD), lambda b, c, pt: (0, past // NEW_PAD)),
                        const(ov), const(gsum)],
            out_specs=pl.BlockSpec((1, n_rows, KV_COLS), lambda b, c, pt: (b, 0, 0)),
            scratch_shapes=[pltpu.VMEM((2, ck, KV_COLS), F32), pltpu.VMEM((2, ck, KV_COLS), F32),
                            pltpu.SemaphoreType.DMA((2, 2)),
                            pltpu.VMEM((n_rows, 1), F32), pltpu.VMEM((n_rows, 1), F32),
                            pltpu.VMEM((n_rows, KV_COLS), F32), pltpu.VMEM((n_rows, nb_pad), F32),
                            pltpu.VMEM((n_rows, KV_COLS), F32)]),
        out_shape=jax.ShapeDtypeStruct((bd, n_rows, KV_COLS), F32),
        compiler_params=_cparams("arbitrary", "arbitrary"),
    )(page_table, flat(pool_k), flat(pool_v), *operands, et, et, ov, gsum)
    o6 = o_big.reshape(bd, N_KV_HEADS, GROUP, s_new, N_KV_HEADS, HEAD_DIM)
    own = jnp.eye(N_KV_HEADS, dtype=F32)[None, :, None, None, :, None]
    o = jnp.sum(o6 * own, axis=4).transpose(0, 3, 1, 2, 4)
    return o.reshape(bd * s_new, Q_COLS).astype(BF16)


def _proj_ln_kernel(alpha, a_ref, w_ref, x_ref, g_ref, b_ref, o_ref):
    h = _dot(a_ref[...], w_ref[...])
    o_ref[...] = _layer_norm(alpha * x_ref[...] + h, g_ref[...], b_ref[...])


def _proj_ln(a, w, x, g, b, alpha):
    n, d = x.shape
    tm = _row_tile(n, 512)
    vec = pl.BlockSpec((1, d), lambda i: (0, 0))
    return pl.pallas_call(
        functools.partial(_proj_ln_kernel, alpha),
        grid=(n // tm,),
        in_specs=[pl.BlockSpec((tm, a.shape[1]), lambda i: (i, 0)), pl.BlockSpec(w.shape, lambda i: (0, 0)),
                  pl.BlockSpec((tm, d), lambda i: (i, 0)), vec, vec],
        out_specs=pl.BlockSpec((tm, d), lambda i: (i, 0)),
        out_shape=jax.ShapeDtypeStruct((n, d), F32),
        compiler_params=_cparams("parallel"),
    )(a, w, x, g, b)


def _conv_layer(yp, ys, state, w_pw1, b_pw1, w_dw, b_dw, cg, cb, w_pw2, b_pw2, g, b, alpha):
    bsz, t, d = yp.shape
    bd, s, _ = ys.shape
    width = w_dw.shape[0]
    wa, wg = w_pw1[:, :d].astype(BF16), w_pw1[:, d:].astype(BF16)
    ba, bg = b_pw1[None, :d], b_pw1[None, d:]
    w2 = w_pw2.astype(BF16)
    row = lambda v: v[None, :]
    tail = (w_dw, row(b_dw), row(cg), row(cb), w2, row(b_pw2), row(g), row(b), alpha)

    u_p = _pw1_glu(yp.reshape(bsz * t, d), wa, wg, ba, bg).reshape(bsz, t, d)
    new_p = _conv2_prompt(u_p, yp, *tail)
    st_p = u_p[:, t - (width - 1):]

    u_s = _pw1_glu(ys.reshape(bd * s, d), wa, wg, ba, bg).reshape(bd, s, d)
    ext = jnp.concatenate([state, u_s], axis=1)
    rows = _round_up(s, 8)
    ext_pad = jnp.pad(ext, ((0, 0), (0, _round_up(width - 1 + rows, 8) - ext.shape[1]), (0, 0)))
    xs_pad = jnp.pad(ys, ((0, 0), (0, rows - s), (0, 0)))
    new_s = _conv2_sample(ext_pad, xs_pad, *tail)[:, :s]
    st_s = ext[:, s:]
    return new_p, new_s, st_p, st_s


def _nsa_layer(yp, ys, pool_kc, pool_vc, pool_ks, pool_vs, kw_buf, vw_buf, page_table,
               w_in, pe, w1, w2, w_o, g, b, alpha):
    bsz, t, d = yp.shape
    bd, s, _ = ys.shape
    page = pool_kc.shape[1]
    past = page_table.shape[1] * page
    w_buf = kw_buf.shape[1]
    in_cols = w_in.shape[1]
    w_pad = jnp.pad(w_in, ((0, 0), (0, Q_COLS + 6 * KV_COLS + LANES - in_cols))).astype(BF16)
    w_o16 = w_o.astype(BF16)
    g, b = g[None, :], b[None, :]
    kv4 = lambda a, nb, nt: a.reshape(nb, nt, N_KV_HEADS, HEAD_DIM)

    tm = _row_tile(t, 256)
    tabs = _rope_tables(jnp.arange(t))
    q, k_c, v_c, k_s, v_s, k_w, v_w, gates = _nsa_proj(yp.reshape(bsz * t, d), w_pad, tabs, t // tm)
    b3 = lambda a: a.reshape(bsz, t, -1)
    grp = lambda a: a.reshape(bsz, t // CMP_STRIDE, CMP_STRIDE * KV_COLS)
    kc = _compress(grp(k_c), pe[0], w1[0], w2[0], True)
    vc = _compress(grp(v_c), pe[1], w1[1], w2[1], False)
    o = _nsa_attention(b3(q), kc, vc, b3(k_s), b3(v_s), b3(k_w), b3(v_w), b3(gates))
    new_p = _proj_ln(o.reshape(bsz * t, Q_COLS), w_o16, yp.reshape(bsz * t, d), g, b, alpha).reshape(bsz, t, d)
    win = lambda a: jnp.concatenate([jnp.zeros((bsz, w_buf, KV_COLS), F32), b3(a)], axis=1)[:, -w_buf:]
    st_p = tuple(kv4(a, bsz, t) for a in (k_c, v_c, k_s, v_s)) + (
        kv4(win(k_w), bsz, w_buf), kv4(win(v_w), bsz, w_buf))

    n_s = bd * s
    tabs_s = tuple(jnp.tile(tt, (bd, 1)) for tt in _rope_tables(past + jnp.arange(s)))
    qs, k_c2, v_c2, k_s2, v_s2, k_w2, v_w2, gates_s = _nsa_proj(ys.reshape(n_s, d), w_pad, tabs_s, 1)
    s3 = lambda a: a.reshape(bd, s, -1)
    n_cmp_s = (past + s - CMP_LEN) // CMP_STRIDE + 1
    assert n_cmp_s <= past // CMP_STRIDE
    kc_s, vc_s = _compress_paged(pool_kc, pool_vc, page_table, pe, w1, w2, n_cmp_s)
    o_s = _attn_sample(qs, kc_s, vc_s, pool_ks, pool_vs, page_table, k_s2, v_s2, kw_buf, vw_buf, k_w2, v_w2,
                       gates_s, s_new=s)
    kw_all = jnp.concatenate([kw_buf.reshape(bd, w_buf, KV_COLS), s3(k_w2)], axis=1)
    vw_all = jnp.concatenate([vw_buf.reshape(bd, w_buf, KV_COLS), s3(v_w2)], axis=1)
    new_s = _proj_ln(o_s, w_o16, ys.reshape(n_s, d), g, b, alpha).reshape(bd, s, d)
    st_s = tuple(kv4(a, bd, s) for a in (k_c2, v_c2, k_s2, v_s2)) + (
        kv4(kw_all[:, -w_buf:], bd, w_buf), kv4(vw_all[:, -w_buf:], bd, w_buf))
    return new_p, new_s, st_p, st_s


def kernel(x_prompt, x_sample, state_conv, cache_k_cmp, cache_v_cmp, cache_k_sel, cache_v_sel, state_k_win, state_v_win, page_table, ln_g, ln_b, conv_w_pw1, conv_b_pw1, conv_w_dw, conv_b_dw, conv_ln_g, conv_ln_b, conv_w_pw2, conv_b_pw2, nsa_w_in, nsa_cmp_pe, nsa_cmp_w1, nsa_cmp_w2, nsa_w_o, ffn_w_gate, ffn_w_up, ffn_w_down, moe_w_router, moe_w_gate, moe_w_up, moe_w_down):
    depth = ln_g.shape[0]
    alpha = (2 * depth) ** 0.25
    bsz, t, d = x_prompt.shape
    bd, s, _ = x_sample.shape
    yp, ys = x_prompt, x_sample
    conv_p, conv_s, nsa_p, nsa_s = [], [], [], []
    for i in range(depth):
        j = i // 2
        if i % 2 == 0:
            yp, ys, st_p, st_s = _conv_layer(
                yp, ys, state_conv[j], conv_w_pw1[j], conv_b_pw1[j], conv_w_dw[j], conv_b_dw[j],
                conv_ln_g[j], conv_ln_b[j], conv_w_pw2[j], conv_b_pw2[j], ln_g[i, 0], ln_b[i, 0], alpha)
            conv_p.append(st_p)
            conv_s.append(st_s)
        else:
            yp, ys, st_p, st_s = _nsa_layer(
                yp, ys, cache_k_cmp[j], cache_v_cmp[j], cache_k_sel[j], cache_v_sel[j],
                state_k_win[j], state_v_win[j], page_table,
                nsa_w_in[j], nsa_cmp_pe[j], nsa_cmp_w1[j], nsa_cmp_w2[j], nsa_w_o[j],
                ln_g[i, 0], ln_b[i, 0], alpha)
            nsa_p.append(st_p)
            nsa_s.append(st_s)
        g2, b2 = ln_g[i, 1][None, :], ln_b[i, 1][None, :]
        xp, xs = yp.reshape(bsz * t, d), ys.reshape(bd * s, d)
        if i % 2 == 0:
            fw = (ffn_w_gate[j].astype(BF16), ffn_w_up[j].astype(BF16), ffn_w_down[j].astype(BF16))
            xp = _ffn_ln(xp, *fw, g2, b2, alpha)
            xs = _ffn_ln(xs, *fw, g2, b2, alpha)
        else:
            mw = (moe_w_gate[j].astype(BF16), moe_w_up[j].astype(BF16), moe_w_down[j].astype(BF16))
            xp = _moe_ln(xp, _router(xp, moe_w_router[j]), *mw, g2, b2, alpha)
            xs = _moe_ln(xs, _router(xs, moe_w_router[j]), *mw, g2, b2, alpha)
        yp, ys = xp.reshape(bsz, t, d), xs.reshape(bd, s, d)
    stack = lambda parts: tuple(jnp.stack(a) for a in zip(*parts))
    return ((yp, ys, jnp.stack(conv_p)) + stack(nsa_p) + (jnp.stack(conv_s),) + stack(nsa_s))
```

```python
import functools

import numpy as np
import jax
import jax.numpy as jnp
from jax import lax
from jax.experimental import pallas as pl
from jax.experimental.pallas import tpu as pltpu

F32 = jnp.float32
BF16 = jnp.bfloat16

N_HEADS = 16
N_KV_HEADS = 4
GROUP = N_HEADS // N_KV_HEADS
HEAD_DIM = 64
ROT_DIM = HEAD_DIM // 4
ROPE_THETA = 500000.0
CMP_LEN = 32
CMP_STRIDE = 16
SEL_BLOCK = 64
N_SELECT = 16
N_LOCAL_SEL = 2
WINDOW = 512
TOP_K = 2
LN_EPS = 1e-5
NEG = -1e30
BIG = 1e30

Q_TILE = 128
KEY_TILE = 512
KV_COLS = N_KV_HEADS * HEAD_DIM
Q_COLS = N_HEADS * HEAD_DIM
LANES = 128
VMEM_LIMIT = 56 * 1024 * 1024


def _cparams(*sem):
    return pltpu.CompilerParams(dimension_semantics=sem, vmem_limit_bytes=VMEM_LIMIT)


def _round_up(a, m):
    return -(-a // m) * m


def _row_tile(n, pref):
    t = min(n, pref)
    while n % t:
        t //= 2
    assert t >= 8 and n % t == 0
    return t


def _sigmoid(v):
    return 1.0 / (1.0 + jnp.exp(-v))


def _silu(v):
    return v * _sigmoid(v)


def _layer_norm(v, g, b):
    mu = jnp.mean(v, axis=-1, keepdims=True)
    d = v - mu
    var = jnp.mean(d * d, axis=-1, keepdims=True)
    return d * lax.rsqrt(var + LN_EPS) * g + b


def _dot(a, b):
    return jnp.dot(a, b, preferred_element_type=F32)


def _split_bf16(v):
    hi = v.astype(BF16)
    lo = (v - hi.astype(F32)).astype(BF16)
    return hi, lo


def _pw1_glu_kernel(x_ref, wa_ref, wg_ref, ba_ref, bg_ref, u_ref):
    x = x_ref[...].astype(BF16)
    a = _dot(x, wa_ref[...]) + ba_ref[...]
    g = _dot(x, wg_ref[...]) + bg_ref[...]
    u_ref[...] = a * _sigmoid(g)


def _pw1_glu(x, wa, wg, ba, bg):
    n, d = x.shape
    tm = _row_tile(n, 512)
    full = lambda i: (0, 0)
    return pl.pallas_call(
        _pw1_glu_kernel,
        grid=(n // tm,),
        in_specs=[pl.BlockSpec((tm, d), lambda i: (i, 0)),
                  pl.BlockSpec(wa.shape, full), pl.BlockSpec(wg.shape, full),
                  pl.BlockSpec(ba.shape, full), pl.BlockSpec(bg.shape, full)],
        out_specs=pl.BlockSpec((tm, d), lambda i: (i, 0)),
        out_shape=jax.ShapeDtypeStruct((n, d), F32),
        compiler_params=_cparams("parallel"),
    )(x, wa, wg, ba, bg)


HALO = 32


def _dw_taps(ext_ref, wdw_ref, rows, width):
    off = HALO - (width - 1)
    acc = wdw_ref[0:1, :] * ext_ref[pl.ds(off, rows), :]
    for k in range(1, width):
        acc = acc + wdw_ref[k:k + 1, :] * ext_ref[pl.ds(off + k, rows), :]
    return acc


def _conv2_prompt_kernel(width, alpha, u_ref, halo_ref, x_ref, wdw_ref, bdw_ref, cg_ref, cb_ref,
                         w2_ref, b2_ref, g_ref, b_ref, o_ref, ext_ref):
    i = pl.program_id(1)
    tm = u_ref.shape[1]

    @pl.when(i == 0)
    def _():
        ext_ref[0:HALO, :] = jnp.zeros((HALO, ext_ref.shape[1]), F32)

    @pl.when(i > 0)
    def _():
        ext_ref[0:HALO, :] = halo_ref[0]

    ext_ref[HALO:, :] = u_ref[0]
    c = _dw_taps(ext_ref, wdw_ref, tm, width) + bdw_ref[...]
    c = _layer_norm(c, cg_ref[...], cb_ref[...])
    h = _dot(_silu(c).astype(BF16), w2_ref[...]) + b2_ref[...]
    o_ref[0] = _layer_norm(alpha * x_ref[0] + h, g_ref[...], b_ref[...])


def _conv2_prompt(u, x, wdw, bdw, cg, cb, w2, b2, g, b, alpha):
    bsz, t, d = u.shape
    width = wdw.shape[0]
    tm = _row_tile(t, 256)
    assert tm % HALO == 0 and width - 1 <= HALO
    r = tm // HALO
    full = lambda bi, i: (0, 0)
    vec = pl.BlockSpec((1, d), full)
    return pl.pallas_call(
        functools.partial(_conv2_prompt_kernel, width, alpha),
        grid=(bsz, t // tm),
        in_specs=[pl.BlockSpec((1, tm, d), lambda bi, i: (bi, i, 0)),
                  pl.BlockSpec((1, HALO, d), lambda bi, i: (bi, jnp.maximum(i * r - 1, 0), 0)),
                  pl.BlockSpec((1, tm, d), lambda bi, i: (bi, i, 0)),
                  pl.BlockSpec(wdw.shape, full), vec, vec, vec,
                  pl.BlockSpec(w2.shape, full), vec, vec, vec],
        out_specs=pl.BlockSpec((1, tm, d), lambda bi, i: (bi, i, 0)),
        out_shape=jax.ShapeDtypeStruct((bsz, t, d), F32),
        scratch_shapes=[pltpu.VMEM((HALO + tm, d), F32)],
        compiler_params=_cparams("parallel", "arbitrary"),
    )(u, u, x, wdw, bdw, cg, cb, w2, b2, g, b)


def _conv2_sample_kernel(width, alpha, ext_ref, x_ref, wdw_ref, bdw_ref, cg_ref, cb_ref,
                         w2_ref, b2_ref, g_ref, b_ref, o_ref):
    bsz, rows, d = x_ref.shape
    acc = wdw_ref[0:1, :][None] * ext_ref[:, pl.ds(0, rows), :]
    for k in range(1, width):
        acc = acc + wdw_ref[k:k + 1, :][None] * ext_ref[:, pl.ds(k, rows), :]
    c = acc.reshape(bsz * rows, d) + bdw_ref[...]
    c = _layer_norm(c, cg_ref[...], cb_ref[...])
    h = _dot(_silu(c).astype(BF16), w2_ref[...]) + b2_ref[...]
    x = x_ref[...].reshape(bsz * rows, d)
    o_ref[...] = _layer_norm(alpha * x + h, g_ref[...], b_ref[...]).reshape(bsz, rows, d)


def _conv2_sample(ext, x, wdw, bdw, cg, cb, w2, b2, g, b, alpha):
    bsz, rows, d = x.shape
    return pl.pallas_call(
        functools.partial(_conv2_sample_kernel, wdw.shape[0], alpha),
        out_shape=jax.ShapeDtypeStruct((bsz, rows, d), F32),
        compiler_params=pltpu.CompilerParams(vmem_limit_bytes=VMEM_LIMIT),
    )(ext, x, wdw, bdw, cg, cb, w2, b2, g, b)


def _ffn_kernel(alpha, x_ref, wg_ref, wu_ref, wd_ref, g_ref, b_ref, o_ref, acc_ref):
    k = pl.program_id(1)

    @pl.when(k == 0)
    def _():
        acc_ref[...] = jnp.zeros(acc_ref.shape, F32)

    x = x_ref[...].astype(BF16)
    h = _silu(_dot(x, wg_ref[...])) * _dot(x, wu_ref[...])
    acc_ref[...] += _dot(h.astype(BF16), wd_ref[...])

    @pl.when(k == pl.num_programs(1) - 1)
    def _():
        o_ref[...] = _layer_norm(alpha * x_ref[...] + acc_ref[...], g_ref[...], b_ref[...])


def _ff_tile(dff):
    for tf in (512, 384, 256, 128):
        if dff % tf == 0:
            return tf
    return dff


def _ffn_ln(x, wg, wu, wd, g, b, alpha):
    n, d = x.shape
    dff = wg.shape[1]
    tm = _row_tile(n, 1024)
    tf = _ff_tile(dff)
    vec = pl.BlockSpec((1, d), lambda i, k: (0, 0))
    return pl.pallas_call(
        functools.partial(_ffn_kernel, alpha),
        grid=(n // tm, dff // tf),
        in_specs=[pl.BlockSpec((tm, d), lambda i, k: (i, 0)),
                  pl.BlockSpec((d, tf), lambda i, k: (0, k)),
                  pl.BlockSpec((d, tf), lambda i, k: (0, k)),
                  pl.BlockSpec((tf, d), lambda i, k: (k, 0)), vec, vec],
        out_specs=pl.BlockSpec((tm, d), lambda i, k: (i, 0)),
        out_shape=jax.ShapeDtypeStruct((n, d), F32),
        scratch_shapes=[pltpu.VMEM((tm, d), F32)],
        compiler_params=_cparams("parallel", "arbitrary"),
    )(x, wg, wu, wd, g, b)


MOE_TILE = 512
MOE_CHUNK = 128
MOE_ROWS = 512
SEG_ALIGN = 16
N_CHUNK = MOE_TILE // MOE_CHUNK


def _router_kernel(n_exp, n_valid, x_ref, wh_ref, wl_ref, comb_ref, cnt_ref):
    xh, xl = _split_bf16(x_ref[...])
    logits = _dot(xh, wh_ref[...]) + (_dot(xh, wl_ref[...]) + _dot(xl, wh_ref[...]))
    lane = lax.broadcasted_iota(jnp.int32, logits.shape, 1)
    logits = jnp.where(lane < n_exp, logits, -jnp.inf)
    m1 = jnp.max(logits, axis=-1, keepdims=True)
    i1 = jnp.min(jnp.where(logits == m1, lane, LANES), axis=-1, keepdims=True)
    rest = jnp.where(lane == i1, -jnp.inf, logits)
    m2 = jnp.max(rest, axis=-1, keepdims=True)
    i2 = jnp.min(jnp.where(rest == m2, lane, LANES), axis=-1, keepdims=True)
    e2 = jnp.exp(m2 - m1)
    w1 = 1.0 / (1.0 + e2)
    w2 = e2 / (1.0 + e2)
    row = pl.program_id(0) * logits.shape[0] + lax.broadcasted_iota(jnp.int32, (logits.shape[0], 1), 0)
    comb = jnp.where(row < n_valid, jnp.where(lane == i1, w1, 0.0) + jnp.where(lane == i2, w2, 0.0), 0.0)
    comb_ref[...] = comb
    cnt_ref[0] = jnp.sum(jnp.where(comb > 0.0, 1.0, 0.0), axis=0, keepdims=True)


def _router(x, w_router, n_valid):
    n, d = x.shape
    n_exp = w_router.shape[1]
    assert TOP_K == 2 and n_exp <= LANES and n % MOE_TILE == 0
    wpad = jnp.pad(w_router, ((0, 0), (0, LANES - n_exp)))
    wh = wpad.astype(BF16)
    wl = (wpad - wh.astype(F32)).astype(BF16)
    n_tiles = n // MOE_TILE
    return pl.pallas_call(
        functools.partial(_router_kernel, n_exp, n_valid),
        grid=(n_tiles,),
        in_specs=[pl.BlockSpec((MOE_TILE, d), lambda i: (i, 0)),
                  pl.BlockSpec((d, LANES), lambda i: (0, 0)),
                  pl.BlockSpec((d, LANES), lambda i: (0, 0))],
        out_specs=[pl.BlockSpec((MOE_TILE, LANES), lambda i: (i, 0)),
                   pl.BlockSpec((1, 1, LANES), lambda i: (i, 0, 0))],
        out_shape=[jax.ShapeDtypeStruct((n, LANES), F32), jax.ShapeDtypeStruct((n_tiles, 1, LANES), F32)],
        compiler_params=_cparams("parallel"),
    )(x, wh, wl)


def _segment_copy(hbm_ref, buf, sems, base_ref, tile, n_exp, e, ch, to_hbm):
    off = pl.multiple_of(base_ref[tile * n_exp + e] + ch * MOE_CHUNK, SEG_ALIGN)
    slot = e * N_CHUNK + ch
    rows = hbm_ref.at[pl.ds(off, MOE_CHUNK)]
    src, dst = (buf.at[slot], rows) if to_hbm else (rows, buf.at[slot])
    return pltpu.make_async_copy(src, dst, sems.at[slot])


def _dispatch_kernel(n_exp, base_ref, cnt_ref, x_ref, comb_ref, upper_ref, zeros_ref, xs_ref, buf, sems):
    del zeros_ref
    tile = pl.program_id(0)
    x16 = x_ref[...].astype(BF16)
    sel_t = jnp.where(comb_ref[...] > 0.0, 1.0, 0.0).T
    rank_t = _dot(sel_t.astype(BF16), upper_ref[...])
    used = lambda e, ch: ch * MOE_CHUNK < cnt_ref[tile * n_exp + e]
    for e in range(n_exp):
        for ch in range(N_CHUNK):
            @pl.when(used(e, ch))
            def _(e=e, ch=ch):
                rows = (ch * MOE_CHUNK + lax.broadcasted_iota(jnp.int32, (MOE_CHUNK, 1), 0)).astype(F32)
                pick = jnp.where(rank_t[e:e + 1] == rows, sel_t[e:e + 1], 0.0).astype(BF16)
                buf[e * N_CHUNK + ch] = _dot(pick, x16).astype(BF16)
                _segment_copy(xs_ref, buf, sems, base_ref, tile, n_exp, e, ch, True).start()
    for e in range(n_exp):
        for ch in range(N_CHUNK):
            @pl.when(used(e, ch))
            def _(e=e, ch=ch):
                _segment_copy(xs_ref, buf, sems, base_ref, tile, n_exp, e, ch, True).wait()


def _expert_ffn_kernel(te_ref, tv_ref, x_ref, wg_ref, wu_ref, wd_ref, o_ref, acc_ref):
    del te_ref
    j, k = pl.program_id(0), pl.program_id(1)

    @pl.when(k == 0)
    def _():
        acc_ref[...] = jnp.zeros(acc_ref.shape, F32)

    @pl.when(tv_ref[j] > 0)
    def _():
        x = x_ref[...]
        h = _silu(_dot(x, wg_ref[0])) * _dot(x, wu_ref[0])
        acc_ref[...] += _dot(h.astype(BF16), wd_ref[0])

    @pl.when(k == pl.num_programs(1) - 1)
    def _():
        o_ref[...] = acc_ref[...].astype(BF16)


def _combine_kernel(n_exp, alpha, base_ref, cnt_ref, x_ref, comb_ref, lower_ref, ys_ref, g_ref, b_ref,
                    o_ref, buf, sems, acc_ref):
    tile = pl.program_id(0)
    used = lambda e, ch: ch * MOE_CHUNK < cnt_ref[tile * n_exp + e]
    for e in range(n_exp):
        for ch in range(N_CHUNK):
            @pl.when(used(e, ch))
            def _(e=e, ch=ch):
                _segment_copy(ys_ref, buf, sems, base_ref, tile, n_exp, e, ch, False).start()
    comb = comb_ref[...]
    sel = jnp.where(comb > 0.0, 1.0, 0.0)
    rank = _dot(lower_ref[...], sel.astype(BF16))
    acc_ref[...] = jnp.zeros(acc_ref.shape, F32)
    for e in range(n_exp):
        for ch in range(N_CHUNK):
            @pl.when(used(e, ch))
            def _(e=e, ch=ch):
                _segment_copy(ys_ref, buf, sems, base_ref, tile, n_exp, e, ch, False).wait()
                cols = (ch * MOE_CHUNK + lax.broadcasted_iota(jnp.int32, (1, MOE_CHUNK), 1)).astype(F32)
                pick = jnp.where(rank[:, e:e + 1] == cols, sel[:, e:e + 1], 0.0).astype(BF16)
                acc_ref[...] += comb[:, e:e + 1] * _dot(pick, buf[e * N_CHUNK + ch])
    o_ref[...] = _layer_norm(alpha * x_ref[...] + acc_ref[...], g_ref[...], b_ref[...])


def _moe_ln(x, w_router, wg, wu, wd, g, b, alpha):
    n_valid, d = x.shape
    n_exp, _, dff = wg.shape
    n = _round_up(n_valid, MOE_TILE)
    n_tiles = n // MOE_TILE
    x = jnp.pad(x, ((0, n - n_valid), (0, 0)))
    comb, cnt = _router(x, w_router, n_valid)

    cnt = cnt[:, 0, :n_exp].astype(jnp.int32)
    seg = (cnt + SEG_ALIGN - 1) // SEG_ALIGN * SEG_ALIGN
    used = jnp.sum(seg, axis=0)
    region = (used + MOE_CHUNK + MOE_ROWS - 1) // MOE_ROWS * MOE_ROWS
    region_end = jnp.cumsum(region)
    seg_base = (region_end - region)[None, :] + jnp.cumsum(seg, axis=0) - seg
    n_rows = _round_up(TOP_K * n + n_tiles * n_exp * (SEG_ALIGN - 1) + n_exp * (MOE_CHUNK + MOE_ROWS), MOE_ROWS)
    n_rt = n_rows // MOE_ROWS
    tile_row0 = jnp.arange(n_rt, dtype=jnp.int32) * MOE_ROWS
    tile_exp = jnp.minimum(jnp.sum(region_end[None, :] <= tile_row0[:, None], axis=1), n_exp - 1).astype(jnp.int32)
    tile_valid = (tile_row0 < (region_end - region + used)[tile_exp]).astype(jnp.int32)
    seg_base = seg_base.reshape(-1).astype(jnp.int32)
    cnt = cnt.reshape(-1)

    t = np.arange(MOE_TILE)
    upper = jnp.asarray(t[:, None] < t[None, :], BF16)
    lower = jnp.asarray(t[None, :] < t[:, None], BF16)
    tile_spec = lambda w: pl.BlockSpec((MOE_TILE, w), lambda i, *_: (i, 0))
    const = lambda a: pl.BlockSpec(a.shape, lambda i, *_: (0,) * a.ndim)
    staging = [pltpu.VMEM((n_exp * N_CHUNK, MOE_CHUNK, d), BF16), pltpu.SemaphoreType.DMA((n_exp * N_CHUNK,))]

    xs = pl.pallas_call(
        functools.partial(_dispatch_kernel, n_exp),
        grid_spec=pltpu.PrefetchScalarGridSpec(
            num_scalar_prefetch=2, grid=(n_tiles,),
            in_specs=[tile_spec(d), tile_spec(LANES), const(upper), pl.BlockSpec(memory_space=pl.ANY)],
            out_specs=pl.BlockSpec(memory_space=pl.ANY),
            scratch_shapes=staging),
        out_shape=jax.ShapeDtypeStruct((n_rows, d), BF16),
        input_output_aliases={5: 0},
        compiler_params=_cparams("arbitrary"),
    )(seg_base, cnt, x, comb, upper, jnp.zeros((n_rows, d), BF16))

    tf = _ff_tile(dff)
    ys = pl.pallas_call(
        _expert_ffn_kernel,
        grid_spec=pltpu.PrefetchScalarGridSpec(
            num_scalar_prefetch=2, grid=(n_rt, dff // tf),
            in_specs=[pl.BlockSpec((MOE_ROWS, d), lambda j, k, te, tv: (j, 0)),
                      pl.BlockSpec((1, d, tf), lambda j, k, te, tv: (te[j], 0, k)),
                      pl.BlockSpec((1, d, tf), lambda j, k, te, tv: (te[j], 0, k)),
                      pl.BlockSpec((1, tf, d), lambda j, k, te, tv: (te[j], k, 0))],
            out_specs=pl.BlockSpec((MOE_ROWS, d), lambda j, k, te, tv: (j, 0)),
            scratch_shapes=[pltpu.VMEM((MOE_ROWS, d), F32)]),
        out_shape=jax.ShapeDtypeStruct((n_rows, d), BF16),
        compiler_params=_cparams("parallel", "arbitrary"),
    )(tile_exp, tile_valid, xs, wg, wu, wd)

    out = pl.pallas_call(
        functools.partial(_combine_kernel, n_exp, alpha),
        grid_spec=pltpu.PrefetchScalarGridSpec(
            num_scalar_prefetch=2, grid=(n_tiles,),
            in_specs=[tile_spec(d), tile_spec(LANES), const(lower), pl.BlockSpec(memory_space=pl.ANY),
                      const(g), const(b)],
            out_specs=tile_spec(d),
            scratch_shapes=staging + [pltpu.VMEM((MOE_TILE, d), F32)]),
        out_shape=jax.ShapeDtypeStruct((n, d), F32),
        compiler_params=_cparams("arbitrary"),
    )(seg_base, cnt, x, comb, lower, ys, g, b)
    return out[:n_valid]


def _rope_chunk(v, cos, sa, sb):
    half = ROT_DIM // 2
    return v * cos + pltpu.roll(v, LANES - half, 1) * sa + pltpu.roll(v, half, 1) * sb


def _nsa_proj_kernel(x_ref, w_ref, cos_ref, sa_ref, sb_ref,
                     q_ref, kc_ref, vc_ref, ks_ref, vs_ref, kw_ref, vw_ref, gate_ref):
    y = _dot(x_ref[...].astype(BF16), w_ref[...])
    cos, sa, sb = cos_ref[...], sa_ref[...], sb_ref[...]
    scale = HEAD_DIM ** -0.5
    for j in range(Q_COLS // LANES):
        q_ref[:, j * LANES:(j + 1) * LANES] = (
            _rope_chunk(y[:, j * LANES:(j + 1) * LANES], cos, sa, sb) * scale).astype(BF16)
    base = Q_COLS
    for idx, (ref, roped) in enumerate(((kc_ref, False), (vc_ref, False), (ks_ref, True),
                                        (vs_ref, False), (kw_ref, True), (vw_ref, False))):
        for j in range(KV_COLS // LANES):
            lo = base + idx * KV_COLS + j * LANES
            v = y[:, lo:lo + LANES]
            ref[:, j * LANES:(j + 1) * LANES] = _rope_chunk(v, cos, sa, sb) if roped else v
    gate_ref[...] = _sigmoid(y[:, base + 6 * KV_COLS:base + 6 * KV_COLS + LANES])


def _rope_tables(pos):
    half = ROT_DIM // 2
    inv_freq = ROPE_THETA ** (-jnp.arange(half, dtype=F32) / half)
    ang = pos.astype(F32)[:, None] * inv_freq[None, :]
    cos, sin = jnp.cos(ang), jnp.sin(ang)
    n = pos.shape[0]
    rest = HEAD_DIM - ROT_DIM
    one = jnp.ones((n, rest), F32)
    zero = jnp.zeros((n, rest), F32)
    zh = jnp.zeros((n, half), F32)
    c = jnp.concatenate([cos, cos, one], axis=1)
    sa = jnp.concatenate([-sin, zh, zero], axis=1)
    sb = jnp.concatenate([zh, sin, zero], axis=1)
    rep = LANES // HEAD_DIM
    return tuple(jnp.tile(t, (1, rep)) for t in (c, sa, sb))


def _nsa_proj(x, w_pad, tables, n_tab_blocks):
    n, d = x.shape
    tm = tables[0].shape[0] // n_tab_blocks
    assert n % tm == 0
    row = lambda i: (i, 0)
    tab = pl.BlockSpec((tm, LANES), lambda i: (i % n_tab_blocks, 0))
    kv_spec = pl.BlockSpec((tm, KV_COLS), row)
    kv_shape = jax.ShapeDtypeStruct((n, KV_COLS), F32)
    return pl.pallas_call(
        _nsa_proj_kernel,
        grid=(n // tm,),
        in_specs=[pl.BlockSpec((tm, d), row), pl.BlockSpec(w_pad.shape, lambda i: (0, 0)), tab, tab, tab],
        out_specs=[pl.BlockSpec((tm, Q_COLS), row)] + [kv_spec] * 6 + [pl.BlockSpec((tm, LANES), row)],
        out_shape=[jax.ShapeDtypeStruct((n, Q_COLS), BF16)] + [kv_shape] * 6
                  + [jax.ShapeDtypeStruct((n, LANES), F32)],
        compiler_params=_cparams("parallel"),
    )(x, w_pad, *tables)


def _compress_kernel(roped, g_ref, pea_ref, peb_ref, wa_ref, wb_ref, w2_ref, cos_ref, sa_ref, sb_ref, o_ref):
    grp = g_ref[0]
    rows = grp.shape[0]
    a = _dot((grp + pea_ref[...]).astype(BF16), wa_ref[...])
    bm = _dot((grp + peb_ref[...]).astype(BF16), wb_ref[...])
    h = a + pltpu.roll(bm, rows - 1, 0)
    out = _dot(_silu(h).astype(BF16), w2_ref[...])
    if roped:
        cos, sa, sb = cos_ref[...], sa_ref[...], sb_ref[...]
        for j in range(KV_COLS // LANES):
            o_ref[0, :, j * LANES:(j + 1) * LANES] = _rope_chunk(out[:, j * LANES:(j + 1) * LANES], cos, sa, sb)
    else:
        o_ref[0] = out


def _compress(groups, pe, w1, w2, roped):
    bsz, ng, gw = groups.shape
    assert CMP_LEN == 2 * CMP_STRIDE and gw == CMP_STRIDE * KV_COLS
    pea, peb, wa, wb, w2b = _compress_weights(pe, w1, w2)
    c_end = jnp.arange(ng) * CMP_STRIDE + (CMP_LEN - 1)
    tables = _rope_tables(c_end)
    full = lambda b: (0, 0)
    tab = pl.BlockSpec((ng, LANES), full)
    return pl.pallas_call(
        functools.partial(_compress_kernel, roped),
        grid=(bsz,),
        in_specs=[pl.BlockSpec((1, ng, gw), lambda b: (b, 0, 0)),
                  pl.BlockSpec((1, gw), full), pl.BlockSpec((1, gw), full),
                  pl.BlockSpec(wa.shape, full), pl.BlockSpec(wb.shape, full), pl.BlockSpec(w2b.shape, full),
                  tab, tab, tab],
        out_specs=pl.BlockSpec((1, ng, KV_COLS), lambda b: (b, 0, 0)),
        out_shape=jax.ShapeDtypeStruct((bsz, ng, KV_COLS), F32),
        compiler_params=_cparams("parallel"),
    )(groups, pea, peb, wa, wb, w2b, *tables)


def _softmax_cols(s, mask, col_valid=None):
    s = jnp.where(mask, s, NEG)
    p = jnp.exp(s - jnp.max(s, axis=0, keepdims=True))
    r = 1.0 / jnp.maximum(jnp.sum(p, axis=0, keepdims=True), 1e-30)
    return p * (r if col_valid is None else jnp.where(col_valid, r, 0.0))


def _attn_kernel(n_cmp, n_blk, qT_ref, kc_ref, vcT_ref, oh_ref, ks_ref, vsT_ref, kw_ref, vwT_ref,
                 gate_ref, ovT_ref, o_ref):
    q0 = pl.program_id(2) * Q_TILE
    qT = qT_ref[0, 0, 0]
    cols = qT.shape[1]
    qpos = q0 + lax.broadcasted_iota(jnp.int32, (1, cols), 1) % Q_TILE
    nb_pad = ovT_ref.shape[0]

    kc = kc_ref[0, 0]
    ncp = kc.shape[0]
    cidx = lax.broadcasted_iota(jnp.int32, (ncp, 1), 0)
    c_end = jnp.where(cidx < n_cmp, cidx * CMP_STRIDE + (CMP_LEN - 1), jnp.iinfo(jnp.int32).max)
    p_c = _softmax_cols(_dot(kc, qT), c_end <= qpos, qpos >= CMP_LEN - 1)
    o_c = _dot(vcT_ref[0, 0], p_c.astype(BF16))

    p_sum = p_c[:, 0:Q_TILE]
    for g in range(1, GROUP):
        p_sum = p_sum + p_c[:, g * Q_TILE:(g + 1) * Q_TILE]
    p_hi, p_lo = _split_bf16(p_sum)
    imp = _dot(ovT_ref[...], p_hi) + _dot(ovT_ref[...], p_lo)

    blk = lax.broadcasted_iota(jnp.int32, (nb_pad, 1), 0)
    qp = qpos[:, 0:Q_TILE]
    lag = qp // SEL_BLOCK - blk
    forced = (blk == 0) | ((lag >= 0) & (lag < N_LOCAL_SEL))
    score = jnp.where(forced, BIG, jnp.where(blk * SEL_BLOCK <= qp, imp, -BIG))
    score = jnp.where(blk < n_blk, score, -jnp.inf)
    bias = jnp.full(score.shape, NEG, F32)
    for _ in range(min(N_SELECT, n_blk)):
        m = jnp.max(score, axis=0, keepdims=True)
        j = jnp.min(jnp.where(score == m, blk, nb_pad), axis=0, keepdims=True)
        hit = blk == j
        bias = jnp.where(hit, 0.0, bias)
        score = jnp.where(hit, -jnp.inf, score)
    bias = bias.astype(BF16)
    q_aug = jnp.concatenate([jnp.concatenate([bias] * GROUP, axis=1), qT], axis=0)

    def scores(kb):
        k0 = pl.multiple_of(kb * KEY_TILE, KEY_TILE)
        keys = jnp.concatenate([oh_ref[pl.ds(k0, KEY_TILE), :], ks_ref[0, 0, pl.ds(k0, KEY_TILE), :]], axis=1)
        kpos = k0 + lax.broadcasted_iota(jnp.int32, (KEY_TILE, 1), 0)
        return jnp.where(kpos <= qpos, _dot(keys, q_aug), NEG)

    def accumulate(kb, s, m, l, acc):
        k0 = pl.multiple_of(kb * KEY_TILE, KEY_TILE)
        m_new = jnp.maximum(m, jnp.max(s, axis=0, keepdims=True))
        alpha = jnp.exp(m - m_new)
        p = jnp.exp(s - m_new)
        l = alpha * l + jnp.sum(p, axis=0, keepdims=True)
        acc = alpha * acc + _dot(vsT_ref[0, 0, :, pl.ds(k0, KEY_TILE)], p.astype(BF16))
        return m_new, l, acc

    def body(kb, carry):
        s, m, l, acc = carry
        return (scores(kb + 1),) + accumulate(kb, s, m, l, acc)

    n_full = q0 // KEY_TILE
    init = (scores(0), jnp.full((1, cols), NEG, F32), jnp.zeros((1, cols), F32),
            jnp.zeros((HEAD_DIM, cols), F32))
    s_last, m_s, l_s, acc_s = lax.fori_loop(0, n_full, body, init)
    _, l_s, acc_s = accumulate(n_full, s_last, m_s, l_s, acc_s)
    o_s = acc_s * (1.0 / jnp.maximum(l_s, 1e-30))

    wlen = WINDOW + Q_TILE
    w0 = pl.multiple_of(jnp.maximum(q0 - WINDOW, 0), Q_TILE)
    wpos = w0 + lax.broadcasted_iota(jnp.int32, (wlen, 1), 0)
    in_window = jnp.abs((2 * qpos - (WINDOW - 1)) - 2 * wpos) <= WINDOW - 1
    p_w = _softmax_cols(_dot(kw_ref[0, 0, pl.ds(w0, wlen), :], qT), in_window)
    o_w = _dot(vwT_ref[0, 0, :, pl.ds(w0, wlen)], p_w.astype(BF16))

    gate = gate_ref[0, 0, 0]
    o_ref[0, 0, 0] = (gate[0:1] * o_c + gate[1:2] * o_s + gate[2:3] * o_w).astype(BF16)


def _overlap_t(n_cmp, n_cmp_pad, n_blk, nb_pad):
    cs = np.arange(n_cmp_pad)[None, :] * CMP_STRIDE
    ss = np.arange(nb_pad)[:, None] * SEL_BLOCK
    ov = np.maximum(np.minimum(cs + CMP_LEN, ss + SEL_BLOCK) - np.maximum(cs, ss), 0) / CMP_LEN
    ov = ov * (np.arange(n_cmp_pad)[None, :] < n_cmp) * (np.arange(nb_pad)[:, None] < n_blk)
    return jnp.asarray(ov, BF16)


def _nsa_attention(q, kc, vc, ks, vs, kw, vw, gates):
    bsz, t, _ = q.shape
    n_qt = t // Q_TILE
    n_cmp = (t - CMP_LEN) // CMP_STRIDE + 1
    n_blk = -(-t // SEL_BLOCK)
    nb_pad = _round_up(n_blk, 16)
    ncp = kc.shape[1]
    cols = GROUP * Q_TILE
    assert t % KEY_TILE == 0 and t >= WINDOW + Q_TILE and n_cmp <= ncp

    heads = lambda a: a.astype(BF16).reshape(bsz, a.shape[1], N_KV_HEADS, HEAD_DIM).transpose(0, 2, 1, 3)
    heads_t = lambda a: a.astype(BF16).reshape(bsz, a.shape[1], N_KV_HEADS, HEAD_DIM).transpose(0, 2, 3, 1)
    qT = q.reshape(bsz, n_qt, Q_TILE, N_KV_HEADS, GROUP, HEAD_DIM).transpose(0, 3, 1, 5, 4, 2)
    qT = qT.reshape(bsz, N_KV_HEADS, n_qt, HEAD_DIM, cols)
    onehot = jnp.asarray(np.arange(t)[:, None] // SEL_BLOCK == np.arange(nb_pad)[None, :], BF16)
    gT = gates[..., :3 * N_HEADS].reshape(bsz, n_qt, Q_TILE, N_KV_HEADS, GROUP, 3).transpose(0, 3, 1, 5, 4, 2)
    gT = jnp.pad(gT.reshape(bsz, N_KV_HEADS, n_qt, 3, cols), ((0, 0), (0, 0), (0, 0), (0, 5), (0, 0)))
    ovT = _overlap_t(n_cmp, ncp, n_blk, nb_pad)

    per_head = lambda *shape: pl.BlockSpec((1, 1) + shape, lambda b, h, i: (b, h, 0, 0))
    per_tile = lambda *shape: pl.BlockSpec((1, 1, 1) + shape, lambda b, h, i: (b, h, i, 0, 0))
    const = lambda a: pl.BlockSpec(a.shape, lambda b, h, i: (0, 0))
    oT = pl.pallas_call(
        functools.partial(_attn_kernel, n_cmp, n_blk),
        grid=(bsz, N_KV_HEADS, n_qt),
        in_specs=[per_tile(HEAD_DIM, cols),
                  per_head(ncp, HEAD_DIM), per_head(HEAD_DIM, ncp),
                  const(onehot), per_head(t, HEAD_DIM), per_head(HEAD_DIM, t),
                  per_head(t, HEAD_DIM), per_head(HEAD_DIM, t),
                  per_tile(8, cols), const(ovT)],
        out_specs=per_tile(HEAD_DIM, cols),
        out_shape=jax.ShapeDtypeStruct((bsz, N_KV_HEADS, n_qt, HEAD_DIM, cols), BF16),
        compiler_params=_cparams("parallel", "parallel", "arbitrary"),
    )(qT, heads(kc), heads_t(vc), onehot, heads(ks), heads_t(vs), heads(kw), heads_t(vw), gT, ovT)
    o = oT.reshape(bsz, N_KV_HEADS, n_qt, HEAD_DIM, GROUP, Q_TILE).transpose(0, 2, 5, 1, 4, 3)
    return o.reshape(bsz, t, Q_COLS)


def _paged_pipeline(pt_ref, pools, bufs, sems, pages_per_chunk, extra_pages, rows_per_page):
    seq, chunk = pl.program_id(0), pl.program_id(1)
    n_chunk = pl.num_programs(1)
    step = seq * n_chunk + chunk
    total = pl.num_programs(0) * n_chunk
    slot = step % 2
    last_page = n_chunk * pages_per_chunk - 1
    n_copy = pages_per_chunk + extra_pages

    def start(sq, ck, sl):
        def body(j, carry):
            page = pt_ref[sq, jnp.minimum(ck * pages_per_chunk + j, last_page)]
            dst = pl.ds(pl.multiple_of(j * rows_per_page, rows_per_page), rows_per_page)
            for i, (pool, buf) in enumerate(zip(pools, bufs)):
                pltpu.make_async_copy(pool.at[page], buf.at[sl, dst], sems.at[i, sl]).start()
            return carry
        lax.fori_loop(0, n_copy, body, 0)

    @pl.when(step == 0)
    def _():
        start(seq, chunk, slot)

    @pl.when(step + 1 < total)
    def _():
        nxt = step + 1
        start(nxt // n_chunk, nxt % n_chunk, 1 - slot)

    def wait_body(j, carry):
        dst = pl.ds(pl.multiple_of(j * rows_per_page, rows_per_page), rows_per_page)
        for i, (pool, buf) in enumerate(zip(pools, bufs)):
            pltpu.make_async_copy(pool.at[0], buf.at[slot, dst], sems.at[i, slot]).wait()
        return carry
    lax.fori_loop(0, n_copy, wait_body, 0)
    return slot


def _compress_paged_kernel(pages_per_chunk, n_cmp, pt_ref, poolk_ref, poolv_ref,
                           peak_ref, pebk_ref, wak_ref, wbk_ref, w2k_ref,
                           peav_ref, pebv_ref, wav_ref, wbv_ref, w2v_ref,
                           cos_ref, sa_ref, sb_ref, ok_ref, ov_ref, kbuf, vbuf, sems):
    gpp = poolk_ref.shape[1]
    rows = pages_per_chunk * gpp
    slot = _paged_pipeline(pt_ref, (poolk_ref, poolv_ref), (kbuf, vbuf), sems, pages_per_chunk, 1, gpp)
    row0 = pl.program_id(1) * rows
    valid = row0 + lax.broadcasted_iota(jnp.int32, (rows, 1), 0) < n_cmp

    def mlp(buf, pea_ref, peb_ref, wa_ref, wb_ref, w2_ref):
        grp = buf[slot]
        a = _dot((grp[:rows] + pea_ref[...]).astype(BF16), wa_ref[...])
        bm = _dot((grp + peb_ref[...]).astype(BF16), wb_ref[...])
        h = a + pltpu.roll(bm, rows + gpp - 1, 0)[:rows]
        return jnp.where(valid, _dot(_silu(h).astype(BF16), w2_ref[...]), 0.0)

    outk = mlp(kbuf, peak_ref, pebk_ref, wak_ref, wbk_ref, w2k_ref)
    cos, sa, sb = cos_ref[...], sa_ref[...], sb_ref[...]
    for j in range(KV_COLS // LANES):
        ok_ref[0, :, j * LANES:(j + 1) * LANES] = _rope_chunk(
            outk[:, j * LANES:(j + 1) * LANES], cos, sa, sb).astype(BF16)
    ov_ref[0] = mlp(vbuf, peav_ref, pebv_ref, wav_ref, wbv_ref, w2v_ref).astype(BF16)


def _compress_weights(pe, w1, w2):
    eye = jnp.eye(N_KV_HEADS, dtype=F32)
    blockdiag = lambda w: jnp.einsum('lde,hg->lhdge', w, eye).reshape(-1, KV_COLS).astype(BF16)
    w2b = jnp.einsum('de,hg->hdge', w2, eye).reshape(KV_COLS, KV_COLS).astype(BF16)
    pe_row = lambda p: jnp.tile(p[:, None, :], (1, N_KV_HEADS, 1)).reshape(1, -1)
    return (pe_row(pe[:CMP_STRIDE]), pe_row(pe[CMP_STRIDE:]),
            blockdiag(w1[:CMP_STRIDE]), blockdiag(w1[CMP_STRIDE:]), w2b)


def _compress_paged(pool_k, pool_v, page_table, pe, w1, w2, n_cmp):
    bd, n_pages = page_table.shape
    page = pool_k.shape[1]
    gpp = page // CMP_STRIDE
    gw = CMP_STRIDE * KV_COLS
    n_chunk = 2 if n_pages % 2 == 0 else 1
    ppc = n_pages // n_chunk
    rows = ppc * gpp
    ng = n_pages * gpp
    as_groups = lambda pool: pool.reshape(pool.shape[0], gpp, gw)
    wk = _compress_weights(pe[0], w1[0], w2[0])
    wv = _compress_weights(pe[1], w1[1], w2[1])
    tables = _rope_tables(jnp.arange(ng) * CMP_STRIDE + (CMP_LEN - 1))
    full = lambda b, c, pt: (0, 0)
    wspecs = [pl.BlockSpec(w.shape, full) for w in wk + wv]
    tab = pl.BlockSpec((rows, LANES), lambda b, c, pt: (c, 0))
    out_spec = pl.BlockSpec((1, rows, KV_COLS), lambda b, c, pt: (b, c, 0))
    out_shape = jax.ShapeDtypeStruct((bd, ng, KV_COLS), BF16)
    return pl.pallas_call(
        functools.partial(_compress_paged_kernel, ppc, n_cmp),
        grid_spec=pltpu.PrefetchScalarGridSpec(
            num_scalar_prefetch=1, grid=(bd, n_chunk),
            in_specs=[pl.BlockSpec(memory_space=pl.ANY), pl.BlockSpec(memory_space=pl.ANY)] + wspecs + [tab] * 3,
            out_specs=[out_spec, out_spec],
            scratch_shapes=[pltpu.VMEM((2, rows + gpp, gw), F32), pltpu.VMEM((2, rows + gpp, gw), F32),
                            pltpu.SemaphoreType.DMA((2, 2))]),
        out_shape=[out_shape, out_shape],
        compiler_params=_cparams("arbitrary", "arbitrary"),
    )(page_table, as_groups(pool_k), as_groups(pool_v), *wk, *wv, *tables)


NEW_PAD = 128
CHUNK_PAGES = 16


def _dot_nt(a, b):
    return lax.dot_general(a, b, (((1,), (1,)), ((), ())), preferred_element_type=F32)


def _softmax_rows2(s1, mask1, s2, mask2):
    s1, s2 = jnp.where(mask1, s1, NEG), jnp.where(mask2, s2, NEG)
    m = jnp.maximum(jnp.max(s1, axis=-1, keepdims=True), jnp.max(s2, axis=-1, keepdims=True))
    p1 = jnp.where(mask1, jnp.exp(s1 - m), 0.0)
    p2 = jnp.where(mask2, jnp.exp(s2 - m), 0.0)
    r = 1.0 / jnp.maximum(jnp.sum(p1, axis=-1, keepdims=True) + jnp.sum(p2, axis=-1, keepdims=True), 1e-30)
    return p1 * r, p2 * r


def _attn_sample_kernel(s_new, past, n_cmp, n_blk, pt_ref, poolk_ref, poolv_ref, q_ref, kc_ref, vc_ref,
                        knew_ref, vnew_ref, kwb_ref, vwb_ref, kwn_ref, vwn_ref, gate_ref,
                        et_ref, ett_ref, ov_ref, gsum_ref, o_ref,
                        kbuf, vbuf, sems, m_ref, l_ref, acc_ref, bias_ref, part_ref):
    page = poolk_ref.shape[1]
    slot = _paged_pipeline(pt_ref, (poolk_ref, poolv_ref), (kbuf, vbuf), sems, CHUNK_PAGES, 0, page)
    chunk = pl.program_id(1)
    q = q_ref[0]
    n_rows = q.shape[0]
    qpos = past + lax.broadcasted_iota(jnp.int32, (n_rows, 1), 0) % s_new
    gate = gate_ref[0]

    @pl.when(chunk == 0)
    def _():
        ncp = kc_ref.shape[1]
        cidx = lax.broadcasted_iota(jnp.int32, (1, ncp), 1)
        cmask = (cidx * CMP_STRIDE + (CMP_LEN - 1) <= qpos) & (cidx < n_cmp)
        s_c = jnp.where(cmask, _dot_nt(q, kc_ref[0]), NEG)
        p_c = jnp.where(cmask, jnp.exp(s_c - jnp.max(s_c, axis=-1, keepdims=True)), 0.0)
        p_c = p_c * (1.0 / jnp.maximum(jnp.sum(p_c, axis=-1, keepdims=True), 1e-30))
        o_c = _dot(p_c.astype(BF16), vc_ref[0])
        p_hi, p_lo = _split_bf16(p_c)
        p_sum = _dot(gsum_ref[...], p_hi) + _dot(gsum_ref[...], p_lo)
        s_hi, s_lo = _split_bf16(p_sum)
        imp = _dot(s_hi, ov_ref[...]) + _dot(s_lo, ov_ref[...])
        nb_pad = imp.shape[1]
        blk = lax.broadcasted_iota(jnp.int32, (1, nb_pad), 1)
        lag = qpos // SEL_BLOCK - blk
        forced = (blk == 0) | ((lag >= 0) & (lag < N_LOCAL_SEL))
        score = jnp.where(forced, BIG, jnp.where(blk * SEL_BLOCK <= qpos, imp, -BIG))
        score = jnp.where(blk < n_blk, score, -jnp.inf)
        bias = jnp.full(score.shape, NEG, F32)
        for _ in range(min(N_SELECT, n_blk)):
            m = jnp.max(score, axis=-1, keepdims=True)
            j = jnp.min(jnp.where(score == m, blk, nb_pad), axis=-1, keepdims=True)
            hit = blk == j
            bias = jnp.where(hit, 0.0, bias)
            score = jnp.where(hit, -jnp.inf, score)
        bias_ref[...] = bias
        w_buf = kwb_ref.shape[1]
        wpos = past - w_buf + lax.broadcasted_iota(jnp.int32, (1, w_buf), 1)
        npos = lax.broadcasted_iota(jnp.int32, (1, NEW_PAD), 1)
        p1, p2 = _softmax_rows2(
            _dot_nt(q, kwb_ref[0].astype(BF16)), (wpos <= qpos) & (wpos > qpos - WINDOW),
            _dot_nt(q, kwn_ref[0]), (past + npos <= qpos) & (past + npos > qpos - WINDOW) & (npos < s_new))
        o_w = _dot(p1.astype(BF16), vwb_ref[0].astype(BF16)) + _dot(p2.astype(BF16), vwn_ref[0])
        part_ref[...] = gate[:, 0:1] * o_c + gate[:, 2:3] * o_w
        m_ref[...] = jnp.full(m_ref.shape, NEG, F32)
        l_ref[...] = jnp.zeros(l_ref.shape, F32)
        acc_ref[...] = jnp.zeros(acc_ref.shape, F32)

    bias16 = bias_ref[...].astype(BF16)

    def flash(s, v):
        m_new = jnp.maximum(m_ref[...], jnp.max(s, axis=-1, keepdims=True))
        alpha = jnp.exp(m_ref[...] - m_new)
        p = jnp.exp(s - m_new)
        l_ref[...] = alpha * l_ref[...] + jnp.sum(p, axis=-1, keepdims=True)
        acc_ref[...] = alpha * acc_ref[...] + _dot(p.astype(BF16), v)
        m_ref[...] = m_new

    flash(_dot_nt(q, kbuf[slot].astype(BF16)) + _dot(bias16, et_ref[...]), vbuf[slot].astype(BF16))

    @pl.when(chunk == pl.num_programs(1) - 1)
    def _():
        npos = lax.broadcasted_iota(jnp.int32, (1, NEW_PAD), 1)
        s_t = _dot_nt(q, knew_ref[0]) + _dot(bias16, ett_ref[...])
        flash(jnp.where((past + npos <= qpos) & (npos < s_new), s_t, NEG), vnew_ref[0])
        o_s = acc_ref[...] * (1.0 / jnp.maximum(l_ref[...], 1e-30))
        o_ref[0] = part_ref[...] + gate[:, 1:2] * o_s


def _attn_sample(qs, kc, vc, pool_k, pool_v, page_table, k_new, v_new, kw_buf, vw_buf, kw_new, vw_new, gates,
                 *, s_new):
    bd, n_pages = page_table.shape
    page = pool_k.shape[1]
    past = n_pages * page
    w_buf = kw_buf.shape[1]
    n_rows = N_HEADS * s_new
    n_cmp = (past + s_new - CMP_LEN) // CMP_STRIDE + 1
    n_blk = -(-(past + s_new) // SEL_BLOCK)
    nb_pad = _round_up(n_blk, 2 * LANES)
    ncp = kc.shape[1]
    assert n_pages % CHUNK_PAGES == 0 and s_new <= NEW_PAD and n_cmp <= ncp and w_buf == min(WINDOW, past)
    ck = CHUNK_PAGES * page
    eye = jnp.eye(N_KV_HEADS, dtype=BF16)

    q5 = qs.reshape(bd, s_new, N_KV_HEADS, GROUP, HEAD_DIM).transpose(0, 2, 3, 1, 4)
    qbig = jnp.einsum('bkgsd,kj->bkgsjd', q5, eye).reshape(bd, n_rows, KV_COLS)
    g5 = gates[:, :3 * N_HEADS].reshape(bd, s_new, N_KV_HEADS, GROUP, 3).transpose(0, 2, 3, 1, 4)
    gate = jnp.pad(g5.reshape(bd, n_rows, 3), ((0, 0), (0, 0), (0, LANES - 3)))
    newrows = lambda a: jnp.pad(a.reshape(bd, s_new, KV_COLS), ((0, 0), (0, NEW_PAD - s_new), (0, 0))).astype(BF16)
    flat = lambda a: a.reshape(a.shape[0], a.shape[1], KV_COLS)

    keyblk = np.arange(past + NEW_PAD)[None, :] // SEL_BLOCK
    et = jnp.asarray(keyblk == np.arange(nb_pad)[:, None], BF16)
    cs = np.arange(ncp)[:, None] * CMP_STRIDE
    ss = np.arange(nb_pad)[None, :] * SEL_BLOCK
    ov = np.maximum(np.minimum(cs + CMP_LEN, ss + SEL_BLOCK) - np.maximum(cs, ss), 0) / CMP_LEN
    ov = jnp.asarray(ov * (np.arange(ncp)[:, None] < n_cmp) * (np.arange(nb_pad)[None, :] < n_blk), BF16)
    r = np.arange(n_rows)
    same = (r[:, None] // (GROUP * s_new) == r[None, :] // (GROUP * s_new)) & (r[:, None] % s_new == r[None, :] % s_new)
    gsum = jnp.asarray(same, BF16)

    per_seq = lambda a: pl.BlockSpec((1,) + a.shape[1:], lambda b, c, pt: (b, 0, 0))
    const = lambda a: pl.BlockSpec(a.shape, lambda b, c, pt: (0, 0))
    operands = [qbig, kc, vc, newrows(k_new), newrows(v_new), flat(kw_buf), flat(vw_buf),
                newrows(kw_new), newrows(vw_new), gate]
    o_big = pl.pallas_call(
        functools.partial(_attn_sample_kernel, s_new, past, n_cmp, n_blk),
        grid_spec=pltpu.PrefetchScalarGridSpec(
            num_scalar_prefetch=1, grid=(bd, n_pages // CHUNK_PAGES),
            in_specs=[pl.BlockSpec(memory_space=pl.ANY), pl.BlockSpec(memory_space=pl.ANY)]
                     + [per_seq(a) for a in operands]
                     + [pl.BlockSpec((nb_pad, ck), lambda b, c, pt: (0, c)),
                        pl.BlockSpec((nb_pad, NEW_PAD), lambda b, c, pt: (0, past // NEW_PAD)),
                        const(ov), const(gsum)],
            out_specs=pl.BlockSpec((1, n_rows, KV_COLS), lambda b, c, pt: (b, 0, 0)),
            scratch_shapes=[pltpu.VMEM((2, ck, KV_COLS), F32), pltpu.VMEM((2, ck, KV_COLS), F32),
                            pltpu.SemaphoreType.DMA((2, 2)),
                            pltpu.VMEM((n_rows, 1), F32), pltpu.VMEM((n_rows, 1), F32),
                            pltpu.VMEM((n_rows, KV_COLS), F32), pltpu.VMEM((n_rows, nb_pad), F32),
                            pltpu.VMEM((n_rows, KV_COLS), F32)]),
        out_shape=jax.ShapeDtypeStruct((bd, n_rows, KV_COLS), F32),
        compiler_params=_cparams("arbitrary", "arbitrary"),
    )(page_table, flat(pool_k), flat(pool_v), *operands, et, et, ov, gsum)
    o6 = o_big.reshape(bd, N_KV_HEADS, GROUP, s_new, N_KV_HEADS, HEAD_DIM)
    own = jnp.eye(N_KV_HEADS, dtype=F32)[None, :, None, None, :, None]
    o = jnp.sum(o6 * own, axis=4).transpose(0, 3, 1, 2, 4)
    return o.reshape(bd * s_new, Q_COLS).astype(BF16)


def _proj_ln_kernel(alpha, a_ref, w_ref, x_ref, g_ref, b_ref, o_ref):
    h = _dot(a_ref[...], w_ref[...])
    o_ref[...] = _layer_norm(alpha * x_ref[...] + h, g_ref[...], b_ref[...])


def _proj_ln(a, w, x, g, b, alpha):
    n, d = x.shape
    tm = _row_tile(n, 512)
    vec = pl.BlockSpec((1, d), lambda i: (0, 0))
    return pl.pallas_call(
        functools.partial(_proj_ln_kernel, alpha),
        grid=(n // tm,),
        in_specs=[pl.BlockSpec((tm, a.shape[1]), lambda i: (i, 0)), pl.BlockSpec(w.shape, lambda i: (0, 0)),
                  pl.BlockSpec((tm, d), lambda i: (i, 0)), vec, vec],
        out_specs=pl.BlockSpec((tm, d), lambda i: (i, 0)),
        out_shape=jax.ShapeDtypeStruct((n, d), F32),
        compiler_params=_cparams("parallel"),
    )(a, w, x, g, b)


def _conv_layer(yp, ys, state, w_pw1, b_pw1, w_dw, b_dw, cg, cb, w_pw2, b_pw2, g, b, alpha):
    bsz, t, d = yp.shape
    bd, s, _ = ys.shape
    width = w_dw.shape[0]
    wa, wg = w_pw1[:, :d].astype(BF16), w_pw1[:, d:].astype(BF16)
    ba, bg = b_pw1[None, :d], b_pw1[None, d:]
    w2 = w_pw2.astype(BF16)
    row = lambda v: v[None, :]
    tail = (w_dw, row(b_dw), row(cg), row(cb), w2, row(b_pw2), row(g), row(b), alpha)

    u_p = _pw1_glu(yp.reshape(bsz * t, d), wa, wg, ba, bg).reshape(bsz, t, d)
    new_p = _conv2_prompt(u_p, yp, *tail)
    st_p = u_p[:, t - (width - 1):]

    u_s = _pw1_glu(ys.reshape(bd * s, d), wa, wg, ba, bg).reshape(bd, s, d)
    ext = jnp.concatenate([state, u_s], axis=1)
    rows = _round_up(s, 8)
    ext_pad = jnp.pad(ext, ((0, 0), (0, _round_up(width - 1 + rows, 8) - ext.shape[1]), (0, 0)))
    xs_pad = jnp.pad(ys, ((0, 0), (0, rows - s), (0, 0)))
    new_s = _conv2_sample(ext_pad, xs_pad, *tail)[:, :s]
    st_s = ext[:, s:]
    return new_p, new_s, st_p, st_s


def _nsa_layer(yp, ys, pool_kc, pool_vc, pool_ks, pool_vs, kw_buf, vw_buf, page_table,
               w_in, pe, w1, w2, w_o, g, b, alpha):
    bsz, t, d = yp.shape
    bd, s, _ = ys.shape
    page = pool_kc.shape[1]
    past = page_table.shape[1] * page
    w_buf = kw_buf.shape[1]
    in_cols = w_in.shape[1]
    w_pad = jnp.pad(w_in, ((0, 0), (0, Q_COLS + 6 * KV_COLS + LANES - in_cols))).astype(BF16)
    w_o16 = w_o.astype(BF16)
    g, b = g[None, :], b[None, :]
    kv4 = lambda a, nb, nt: a.reshape(nb, nt, N_KV_HEADS, HEAD_DIM)

    tm = _row_tile(t, 256)
    tabs = _rope_tables(jnp.arange(t))
    q, k_c, v_c, k_s, v_s, k_w, v_w, gates = _nsa_proj(yp.reshape(bsz * t, d), w_pad, tabs, t // tm)
    b3 = lambda a: a.reshape(bsz, t, -1)
    grp = lambda a: a.reshape(bsz, t // CMP_STRIDE, CMP_STRIDE * KV_COLS)
    kc = _compress(grp(k_c), pe[0], w1[0], w2[0], True)
    vc = _compress(grp(v_c), pe[1], w1[1], w2[1], False)
    o = _nsa_attention(b3(q), kc, vc, b3(k_s), b3(v_s), b3(k_w), b3(v_w), b3(gates))
    new_p = _proj_ln(o.reshape(bsz * t, Q_COLS), w_o16, yp.reshape(bsz * t, d), g, b, alpha).reshape(bsz, t, d)
    win = lambda a: jnp.concatenate([jnp.zeros((bsz, w_buf, KV_COLS), F32), b3(a)], axis=1)[:, -w_buf:]
    st_p = tuple(kv4(a, bsz, t) for a in (k_c, v_c, k_s, v_s)) + (
        kv4(win(k_w), bsz, w_buf), kv4(win(v_w), bsz, w_buf))

    n_s = bd * s
    tabs_s = tuple(jnp.tile(tt, (bd, 1)) for tt in _rope_tables(past + jnp.arange(s)))
    qs, k_c2, v_c2, k_s2, v_s2, k_w2, v_w2, gates_s = _nsa_proj(ys.reshape(n_s, d), w_pad, tabs_s, 1)
    s3 = lambda a: a.reshape(bd, s, -1)
    n_cmp_s = (past + s - CMP_LEN) // CMP_STRIDE + 1
    assert n_cmp_s <= past // CMP_STRIDE
    kc_s, vc_s = _compress_paged(pool_kc, pool_vc, page_table, pe, w1, w2, n_cmp_s)
    o_s = _attn_sample(qs, kc_s, vc_s, pool_ks, pool_vs, page_table, k_s2, v_s2, kw_buf, vw_buf, k_w2, v_w2,
                       gates_s, s_new=s)
    kw_all = jnp.concatenate([kw_buf.reshape(bd, w_buf, KV_COLS), s3(k_w2)], axis=1)
    vw_all = jnp.concatenate([vw_buf.reshape(bd, w_buf, KV_COLS), s3(v_w2)], axis=1)
    new_s = _proj_ln(o_s, w_o16, ys.reshape(n_s, d), g, b, alpha).reshape(bd, s, d)
    st_s = tuple(kv4(a, bd, s) for a in (k_c2, v_c2, k_s2, v_s2)) + (
        kv4(kw_all[:, -w_buf:], bd, w_buf), kv4(vw_all[:, -w_buf:], bd, w_buf))
    return new_p, new_s, st_p, st_s


def kernel(x_prompt, x_sample, state_conv, cache_k_cmp, cache_v_cmp, cache_k_sel, cache_v_sel, state_k_win, state_v_win, page_table, ln_g, ln_b, conv_w_pw1, conv_b_pw1, conv_w_dw, conv_b_dw, conv_ln_g, conv_ln_b, conv_w_pw2, conv_b_pw2, nsa_w_in, nsa_cmp_pe, nsa_cmp_w1, nsa_cmp_w2, nsa_w_o, ffn_w_gate, ffn_w_up, ffn_w_down, moe_w_router, moe_w_gate, moe_w_up, moe_w_down):
    depth = ln_g.shape[0]
    alpha = (2 * depth) ** 0.25
    bsz, t, d = x_prompt.shape
    bd, s, _ = x_sample.shape
    yp, ys = x_prompt, x_sample
    conv_p, conv_s, nsa_p, nsa_s = [], [], [], []
    for i in range(depth):
        j = i // 2
        if i % 2 == 0:
            yp, ys, st_p, st_s = _conv_layer(
                yp, ys, state_conv[j], conv_w_pw1[j], conv_b_pw1[j], conv_w_dw[j], conv_b_dw[j],
                conv_ln_g[j], conv_ln_b[j], conv_w_pw2[j], conv_b_pw2[j], ln_g[i, 0], ln_b[i, 0], alpha)
            conv_p.append(st_p)
            conv_s.append(st_s)
        else:
            yp, ys, st_p, st_s = _nsa_layer(
                yp, ys, cache_k_cmp[j], cache_v_cmp[j], cache_k_sel[j], cache_v_sel[j],
                state_k_win[j], state_v_win[j], page_table,
                nsa_w_in[j], nsa_cmp_pe[j], nsa_cmp_w1[j], nsa_cmp_w2[j], nsa_w_o[j],
                ln_g[i, 0], ln_b[i, 0], alpha)
            nsa_p.append(st_p)
            nsa_s.append(st_s)
        g2, b2 = ln_g[i, 1][None, :], ln_b[i, 1][None, :]
        xp, xs = yp.reshape(bsz * t, d), ys.reshape(bd * s, d)
        if i % 2 == 0:
            fw = (ffn_w_gate[j].astype(BF16), ffn_w_up[j].astype(BF16), ffn_w_down[j].astype(BF16))
            xp = _ffn_ln(xp, *fw, g2, b2, alpha)
            xs = _ffn_ln(xs, *fw, g2, b2, alpha)
        else:
            mw = (moe_w_gate[j].astype(BF16), moe_w_up[j].astype(BF16), moe_w_down[j].astype(BF16))
            both = _moe_ln(jnp.concatenate([xp, xs], axis=0), moe_w_router[j], *mw, g2, b2, alpha)
            xp, xs = both[:bsz * t], both[bsz * t:]
        yp, ys = xp.reshape(bsz, t, d), xs.reshape(bd, s, d)
    stack = lambda parts: tuple(jnp.stack(a) for a in zip(*parts))
    return ((yp, ys, jnp.stack(conv_p)) + stack(nsa_p) + (jnp.stack(conv_s),) + stack(nsa_s))
```

```python
import functools

import numpy as np
import jax
import jax.numpy as jnp
from jax import lax
from jax.experimental import pallas as pl
from jax.experimental.pallas import tpu as pltpu

F32 = jnp.float32
BF16 = jnp.bfloat16

N_HEADS = 16
N_KV_HEADS = 4
GROUP = N_HEADS // N_KV_HEADS
HEAD_DIM = 64
ROT_DIM = HEAD_DIM // 4
ROPE_THETA = 500000.0
CMP_LEN = 32
CMP_STRIDE = 16
SEL_BLOCK = 64
N_SELECT = 16
N_LOCAL_SEL = 2
WINDOW = 512
TOP_K = 2
LN_EPS = 1e-5
NEG = -1e30
BIG = 1e30

Q_TILE = 128
KEY_TILE = 512
KV_COLS = N_KV_HEADS * HEAD_DIM
Q_COLS = N_HEADS * HEAD_DIM
LANES = 128
VMEM_LIMIT = 56 * 1024 * 1024


def _cparams(*sem):
    return pltpu.CompilerParams(dimension_semantics=sem, vmem_limit_bytes=VMEM_LIMIT)


def _round_up(a, m):
    return -(-a // m) * m


def _row_tile(n, pref):
    t = min(n, pref)
    while n % t:
        t //= 2
    assert t >= 8 and n % t == 0
    return t


def _sigmoid(v):
    return 1.0 / (1.0 + jnp.exp(-v))


def _silu(v):
    return v * _sigmoid(v)


def _layer_norm(v, g, b):
    mu = jnp.mean(v, axis=-1, keepdims=True)
    d = v - mu
    var = jnp.mean(d * d, axis=-1, keepdims=True)
    return d * lax.rsqrt(var + LN_EPS) * g + b


def _dot(a, b):
    return jnp.dot(a, b, preferred_element_type=F32)


def _split_bf16(v):
    hi = v.astype(BF16)
    lo = (v - hi.astype(F32)).astype(BF16)
    return hi, lo


def _pw1_glu_kernel(x_ref, wa_ref, wg_ref, ba_ref, bg_ref, u_ref):
    x = x_ref[...].astype(BF16)
    a = _dot(x, wa_ref[...]) + ba_ref[...]
    g = _dot(x, wg_ref[...]) + bg_ref[...]
    u_ref[...] = a * _sigmoid(g)


def _pw1_glu(x, wa, wg, ba, bg):
    n, d = x.shape
    tm = _row_tile(n, 512)
    full = lambda i: (0, 0)
    return pl.pallas_call(
        _pw1_glu_kernel,
        grid=(n // tm,),
        in_specs=[pl.BlockSpec((tm, d), lambda i: (i, 0)),
                  pl.BlockSpec(wa.shape, full), pl.BlockSpec(wg.shape, full),
                  pl.BlockSpec(ba.shape, full), pl.BlockSpec(bg.shape, full)],
        out_specs=pl.BlockSpec((tm, d), lambda i: (i, 0)),
        out_shape=jax.ShapeDtypeStruct((n, d), F32),
        compiler_params=_cparams("parallel"),
    )(x, wa, wg, ba, bg)


HALO = 32


def _dw_taps(ext_ref, wdw_ref, rows, width):
    off = HALO - (width - 1)
    acc = wdw_ref[0:1, :] * ext_ref[pl.ds(off, rows), :]
    for k in range(1, width):
        acc = acc + wdw_ref[k:k + 1, :] * ext_ref[pl.ds(off + k, rows), :]
    return acc


def _conv2_prompt_kernel(width, alpha, u_ref, halo_ref, x_ref, wdw_ref, bdw_ref, cg_ref, cb_ref,
                         w2_ref, b2_ref, g_ref, b_ref, o_ref, ext_ref):
    i = pl.program_id(1)
    tm = u_ref.shape[1]

    @pl.when(i == 0)
    def _():
        ext_ref[0:HALO, :] = jnp.zeros((HALO, ext_ref.shape[1]), F32)

    @pl.when(i > 0)
    def _():
        ext_ref[0:HALO, :] = halo_ref[0]

    ext_ref[HALO:, :] = u_ref[0]
    c = _dw_taps(ext_ref, wdw_ref, tm, width) + bdw_ref[...]
    c = _layer_norm(c, cg_ref[...], cb_ref[...])
    h = _dot(_silu(c).astype(BF16), w2_ref[...]) + b2_ref[...]
    o_ref[0] = _layer_norm(alpha * x_ref[0] + h, g_ref[...], b_ref[...])


def _conv2_prompt(u, x, wdw, bdw, cg, cb, w2, b2, g, b, alpha):
    bsz, t, d = u.shape
    width = wdw.shape[0]
    tm = _row_tile(t, 256)
    assert tm % HALO == 0 and width - 1 <= HALO
    r = tm // HALO
    full = lambda bi, i: (0, 0)
    vec = pl.BlockSpec((1, d), full)
    return pl.pallas_call(
        functools.partial(_conv2_prompt_kernel, width, alpha),
        grid=(bsz, t // tm),
        in_specs=[pl.BlockSpec((1, tm, d), lambda bi, i: (bi, i, 0)),
                  pl.BlockSpec((1, HALO, d), lambda bi, i: (bi, jnp.maximum(i * r - 1, 0), 0)),
                  pl.BlockSpec((1, tm, d), lambda bi, i: (bi, i, 0)),
                  pl.BlockSpec(wdw.shape, full), vec, vec, vec,
                  pl.BlockSpec(w2.shape, full), vec, vec, vec],
        out_specs=pl.BlockSpec((1, tm, d), lambda bi, i: (bi, i, 0)),
        out_shape=jax.ShapeDtypeStruct((bsz, t, d), F32),
        scratch_shapes=[pltpu.VMEM((HALO + tm, d), F32)],
        compiler_params=_cparams("parallel", "arbitrary"),
    )(u, u, x, wdw, bdw, cg, cb, w2, b2, g, b)


def _conv2_sample_kernel(width, alpha, ext_ref, x_ref, wdw_ref, bdw_ref, cg_ref, cb_ref,
                         w2_ref, b2_ref, g_ref, b_ref, o_ref):
    bsz, rows, d = x_ref.shape
    acc = wdw_ref[0:1, :][None] * ext_ref[:, pl.ds(0, rows), :]
    for k in range(1, width):
        acc = acc + wdw_ref[k:k + 1, :][None] * ext_ref[:, pl.ds(k, rows), :]
    c = acc.reshape(bsz * rows, d) + bdw_ref[...]
    c = _layer_norm(c, cg_ref[...], cb_ref[...])
    h = _dot(_silu(c).astype(BF16), w2_ref[...]) + b2_ref[...]
    x = x_ref[...].reshape(bsz * rows, d)
    o_ref[...] = _layer_norm(alpha * x + h, g_ref[...], b_ref[...]).reshape(bsz, rows, d)


def _conv2_sample(ext, x, wdw, bdw, cg, cb, w2, b2, g, b, alpha):
    bsz, rows, d = x.shape
    return pl.pallas_call(
        functools.partial(_conv2_sample_kernel, wdw.shape[0], alpha),
        out_shape=jax.ShapeDtypeStruct((bsz, rows, d), F32),
        compiler_params=pltpu.CompilerParams(vmem_limit_bytes=VMEM_LIMIT),
    )(ext, x, wdw, bdw, cg, cb, w2, b2, g, b)


def _ffn_kernel(alpha, x_ref, wg_ref, wu_ref, wd_ref, g_ref, b_ref, o_ref, acc_ref):
    k = pl.program_id(1)

    @pl.when(k == 0)
    def _():
        acc_ref[...] = jnp.zeros(acc_ref.shape, F32)

    x = x_ref[...].astype(BF16)
    h = _silu(_dot(x, wg_ref[...])) * _dot(x, wu_ref[...])
    acc_ref[...] += _dot(h.astype(BF16), wd_ref[...])

    @pl.when(k == pl.num_programs(1) - 1)
    def _():
        o_ref[...] = _layer_norm(alpha * x_ref[...] + acc_ref[...], g_ref[...], b_ref[...])


def _ff_tile(dff):
    for tf in (512, 384, 256, 128):
        if dff % tf == 0:
            return tf
    return dff


def _ffn_ln(x, wg, wu, wd, g, b, alpha):
    n, d = x.shape
    dff = wg.shape[1]
    tm = _row_tile(n, 1024)
    tf = _ff_tile(dff)
    vec = pl.BlockSpec((1, d), lambda i, k: (0, 0))
    return pl.pallas_call(
        functools.partial(_ffn_kernel, alpha),
        grid=(n // tm, dff // tf),
        in_specs=[pl.BlockSpec((tm, d), lambda i, k: (i, 0)),
                  pl.BlockSpec((d, tf), lambda i, k: (0, k)),
                  pl.BlockSpec((d, tf), lambda i, k: (0, k)),
                  pl.BlockSpec((tf, d), lambda i, k: (k, 0)), vec, vec],
        out_specs=pl.BlockSpec((tm, d), lambda i, k: (i, 0)),
        out_shape=jax.ShapeDtypeStruct((n, d), F32),
        scratch_shapes=[pltpu.VMEM((tm, d), F32)],
        compiler_params=_cparams("parallel", "arbitrary"),
    )(x, wg, wu, wd, g, b)


MOE_TILE = 512
MOE_CHUNK = 128
MOE_ROWS = 512
SEG_ALIGN = 16
N_CHUNK = MOE_TILE // MOE_CHUNK


def _router_kernel(n_exp, n_valid, x_ref, wh_ref, wl_ref, comb_ref, cnt_ref):
    xh, xl = _split_bf16(x_ref[...])
    logits = _dot(xh, wh_ref[...]) + (_dot(xh, wl_ref[...]) + _dot(xl, wh_ref[...]))
    lane = lax.broadcasted_iota(jnp.int32, logits.shape, 1)
    logits = jnp.where(lane < n_exp, logits, -jnp.inf)
    m1 = jnp.max(logits, axis=-1, keepdims=True)
    i1 = jnp.min(jnp.where(logits == m1, lane, LANES), axis=-1, keepdims=True)
    rest = jnp.where(lane == i1, -jnp.inf, logits)
    m2 = jnp.max(rest, axis=-1, keepdims=True)
    i2 = jnp.min(jnp.where(rest == m2, lane, LANES), axis=-1, keepdims=True)
    e2 = jnp.exp(m2 - m1)
    w1 = 1.0 / (1.0 + e2)
    w2 = e2 / (1.0 + e2)
    row = pl.program_id(0) * logits.shape[0] + lax.broadcasted_iota(jnp.int32, (logits.shape[0], 1), 0)
    comb = jnp.where(row < n_valid, jnp.where(lane == i1, w1, 0.0) + jnp.where(lane == i2, w2, 0.0), 0.0)
    comb_ref[...] = comb
    cnt_ref[0] = jnp.sum(jnp.where(comb > 0.0, 1.0, 0.0), axis=0, keepdims=True)


def _router(x, w_router, n_valid):
    n, d = x.shape
    n_exp = w_router.shape[1]
    assert TOP_K == 2 and n_exp <= LANES and n % MOE_TILE == 0
    wpad = jnp.pad(w_router, ((0, 0), (0, LANES - n_exp)))
    wh = wpad.astype(BF16)
    wl = (wpad - wh.astype(F32)).astype(BF16)
    n_tiles = n // MOE_TILE
    return pl.pallas_call(
        functools.partial(_router_kernel, n_exp, n_valid),
        grid=(n_tiles,),
        in_specs=[pl.BlockSpec((MOE_TILE, d), lambda i: (i, 0)),
                  pl.BlockSpec((d, LANES), lambda i: (0, 0)),
                  pl.BlockSpec((d, LANES), lambda i: (0, 0))],
        out_specs=[pl.BlockSpec((MOE_TILE, LANES), lambda i: (i, 0)),
                   pl.BlockSpec((1, 1, LANES), lambda i: (i, 0, 0))],
        out_shape=[jax.ShapeDtypeStruct((n, LANES), F32), jax.ShapeDtypeStruct((n_tiles, 1, LANES), F32)],
        compiler_params=_cparams("parallel"),
    )(x, wh, wl)


def _segment_copy(hbm_ref, buf, sems, base_ref, tile, n_exp, e, ch, to_hbm):
    off = pl.multiple_of(base_ref[tile * n_exp + e] + ch * MOE_CHUNK, SEG_ALIGN)
    slot = e * N_CHUNK + ch
    rows = hbm_ref.at[pl.ds(off, MOE_CHUNK)]
    src, dst = (buf.at[slot], rows) if to_hbm else (rows, buf.at[slot])
    return pltpu.make_async_copy(src, dst, sems.at[slot])


def _dispatch_kernel(n_exp, base_ref, cnt_ref, x_ref, comb_ref, upper_ref, zeros_ref, xs_ref, buf, sems):
    del zeros_ref
    tile = pl.program_id(0)
    x16 = x_ref[...].astype(BF16)
    sel_t = jnp.where(comb_ref[...] > 0.0, 1.0, 0.0).T
    rank_t = _dot(sel_t.astype(BF16), upper_ref[...])
    used = lambda e, ch: ch * MOE_CHUNK < cnt_ref[tile * n_exp + e]
    for e in range(n_exp):
        for ch in range(N_CHUNK):
            @pl.when(used(e, ch))
            def _(e=e, ch=ch):
                rows = (ch * MOE_CHUNK + lax.broadcasted_iota(jnp.int32, (MOE_CHUNK, 1), 0)).astype(F32)
                pick = jnp.where(rank_t[e:e + 1] == rows, sel_t[e:e + 1], 0.0).astype(BF16)
                buf[e * N_CHUNK + ch] = _dot(pick, x16).astype(BF16)
                _segment_copy(xs_ref, buf, sems, base_ref, tile, n_exp, e, ch, True).start()
    for e in range(n_exp):
        for ch in range(N_CHUNK):
            @pl.when(used(e, ch))
            def _(e=e, ch=ch):
                _segment_copy(xs_ref, buf, sems, base_ref, tile, n_exp, e, ch, True).wait()


def _expert_ffn_kernel(te_ref, tv_ref, x_ref, wg_ref, wu_ref, wd_ref, o_ref, acc_ref):
    del te_ref
    j, k = pl.program_id(0), pl.program_id(1)

    @pl.when(k == 0)
    def _():
        acc_ref[...] = jnp.zeros(acc_ref.shape, F32)

    @pl.when(tv_ref[j] > 0)
    def _():
        x = x_ref[...]
        h = _silu(_dot(x, wg_ref[0])) * _dot(x, wu_ref[0])
        acc_ref[...] += _dot(h.astype(BF16), wd_ref[0])

    @pl.when(k == pl.num_programs(1) - 1)
    def _():
        o_ref[...] = acc_ref[...].astype(BF16)


def _combine_kernel(n_exp, alpha, base_ref, cnt_ref, x_ref, comb_ref, lower_ref, ys_ref, g_ref, b_ref,
                    o_ref, buf, sems, acc_ref):
    tile = pl.program_id(0)
    used = lambda e, ch: ch * MOE_CHUNK < cnt_ref[tile * n_exp + e]
    for e in range(n_exp):
        for ch in range(N_CHUNK):
            @pl.when(used(e, ch))
            def _(e=e, ch=ch):
                _segment_copy(ys_ref, buf, sems, base_ref, tile, n_exp, e, ch, False).start()
    comb = comb_ref[...]
    sel = jnp.where(comb > 0.0, 1.0, 0.0)
    rank = _dot(lower_ref[...], sel.astype(BF16))
    acc_ref[...] = jnp.zeros(acc_ref.shape, F32)
    for e in range(n_exp):
        for ch in range(N_CHUNK):
            @pl.when(used(e, ch))
            def _(e=e, ch=ch):
                _segment_copy(ys_ref, buf, sems, base_ref, tile, n_exp, e, ch, False).wait()
                cols = (ch * MOE_CHUNK + lax.broadcasted_iota(jnp.int32, (1, MOE_CHUNK), 1)).astype(F32)
                pick = jnp.where(rank[:, e:e + 1] == cols, sel[:, e:e + 1], 0.0).astype(BF16)
                acc_ref[...] += comb[:, e:e + 1] * _dot(pick, buf[e * N_CHUNK + ch])
    o_ref[...] = _layer_norm(alpha * x_ref[...] + acc_ref[...], g_ref[...], b_ref[...])


def _moe_ln(x, w_router, wg, wu, wd, g, b, alpha):
    n_valid, d = x.shape
    n_exp, _, dff = wg.shape
    n = _round_up(n_valid, MOE_TILE)
    n_tiles = n // MOE_TILE
    x = jnp.pad(x, ((0, n - n_valid), (0, 0)))
    comb, cnt = _router(x, w_router, n_valid)

    cnt = cnt[:, 0, :n_exp].astype(jnp.int32)
    seg = (cnt + SEG_ALIGN - 1) // SEG_ALIGN * SEG_ALIGN
    used = jnp.sum(seg, axis=0)
    region = (used + MOE_CHUNK + MOE_ROWS - 1) // MOE_ROWS * MOE_ROWS
    region_end = jnp.cumsum(region)
    seg_base = (region_end - region)[None, :] + jnp.cumsum(seg, axis=0) - seg
    n_rows = _round_up(TOP_K * n + n_tiles * n_exp * (SEG_ALIGN - 1) + n_exp * (MOE_CHUNK + MOE_ROWS), MOE_ROWS)
    n_rt = n_rows // MOE_ROWS
    tile_row0 = jnp.arange(n_rt, dtype=jnp.int32) * MOE_ROWS
    tile_exp = jnp.minimum(jnp.sum(region_end[None, :] <= tile_row0[:, None], axis=1), n_exp - 1).astype(jnp.int32)
    tile_valid = (tile_row0 < (region_end - region + used)[tile_exp]).astype(jnp.int32)
    seg_base = seg_base.reshape(-1).astype(jnp.int32)
    cnt = cnt.reshape(-1)

    t = np.arange(MOE_TILE)
    upper = jnp.asarray(t[:, None] < t[None, :], BF16)
    lower = jnp.asarray(t[None, :] < t[:, None], BF16)
    tile_spec = lambda w: pl.BlockSpec((MOE_TILE, w), lambda i, *_: (i, 0))
    const = lambda a: pl.BlockSpec(a.shape, lambda i, *_: (0,) * a.ndim)
    staging = [pltpu.VMEM((n_exp * N_CHUNK, MOE_CHUNK, d), BF16), pltpu.SemaphoreType.DMA((n_exp * N_CHUNK,))]

    xs = pl.pallas_call(
        functools.partial(_dispatch_kernel, n_exp),
        grid_spec=pltpu.PrefetchScalarGridSpec(
            num_scalar_prefetch=2, grid=(n_tiles,),
            in_specs=[tile_spec(d), tile_spec(LANES), const(upper), pl.BlockSpec(memory_space=pl.ANY)],
            out_specs=pl.BlockSpec(memory_space=pl.ANY),
            scratch_shapes=staging),
        out_shape=jax.ShapeDtypeStruct((n_rows, d), BF16),
        input_output_aliases={5: 0},
        compiler_params=_cparams("arbitrary"),
    )(seg_base, cnt, x, comb, upper, jnp.zeros((n_rows, d), BF16))

    tf = _ff_tile(dff)
    ys = pl.pallas_call(
        _expert_ffn_kernel,
        grid_spec=pltpu.PrefetchScalarGridSpec(
            num_scalar_prefetch=2, grid=(n_rt, dff // tf),
            in_specs=[pl.BlockSpec((MOE_ROWS, d), lambda j, k, te, tv: (j, 0)),
                      pl.BlockSpec((1, d, tf), lambda j, k, te, tv: (te[j], 0, k)),
                      pl.BlockSpec((1, d, tf), lambda j, k, te, tv: (te[j], 0, k)),
                      pl.BlockSpec((1, tf, d), lambda j, k, te, tv: (te[j], k, 0))],
            out_specs=pl.BlockSpec((MOE_ROWS, d), lambda j, k, te, tv: (j, 0)),
            scratch_shapes=[pltpu.VMEM((MOE_ROWS, d), F32)]),
        out_shape=jax.ShapeDtypeStruct((n_rows, d), BF16),
        compiler_params=_cparams("parallel", "arbitrary"),
    )(tile_exp, tile_valid, xs, wg, wu, wd)

    out = pl.pallas_call(
        functools.partial(_combine_kernel, n_exp, alpha),
        grid_spec=pltpu.PrefetchScalarGridSpec(
            num_scalar_prefetch=2, grid=(n_tiles,),
            in_specs=[tile_spec(d), tile_spec(LANES), const(lower), pl.BlockSpec(memory_space=pl.ANY),
                      const(g), const(b)],
            out_specs=tile_spec(d),
            scratch_shapes=staging + [pltpu.VMEM((MOE_TILE, d), F32)]),
        out_shape=jax.ShapeDtypeStruct((n, d), F32),
        compiler_params=_cparams("arbitrary"),
    )(seg_base, cnt, x, comb, lower, ys, g, b)
    return out[:n_valid]


def _rope_chunk(v, cos, sa, sb):
    half = ROT_DIM // 2
    return v * cos + pltpu.roll(v, LANES - half, 1) * sa + pltpu.roll(v, half, 1) * sb


def _nsa_proj_kernel(x_ref, w_ref, cos_ref, sa_ref, sb_ref,
                     q_ref, kc_ref, vc_ref, ks_ref, vs_ref, kw_ref, vw_ref, gate_ref):
    y = _dot(x_ref[...].astype(BF16), w_ref[...])
    cos, sa, sb = cos_ref[...], sa_ref[...], sb_ref[...]
    scale = HEAD_DIM ** -0.5
    for j in range(Q_COLS // LANES):
        q_ref[:, j * LANES:(j + 1) * LANES] = (
            _rope_chunk(y[:, j * LANES:(j + 1) * LANES], cos, sa, sb) * scale).astype(BF16)
    base = Q_COLS
    for idx, (ref, roped) in enumerate(((kc_ref, False), (vc_ref, False), (ks_ref, True),
                                        (vs_ref, False), (kw_ref, True), (vw_ref, False))):
        for j in range(KV_COLS // LANES):
            lo = base + idx * KV_COLS + j * LANES
            v = y[:, lo:lo + LANES]
            ref[:, j * LANES:(j + 1) * LANES] = _rope_chunk(v, cos, sa, sb) if roped else v
    gate_ref[...] = _sigmoid(y[:, base + 6 * KV_COLS:base + 6 * KV_COLS + LANES])


def _rope_tables(pos):
    half = ROT_DIM // 2
    inv_freq = ROPE_THETA ** (-jnp.arange(half, dtype=F32) / half)
    ang = pos.astype(F32)[:, None] * inv_freq[None, :]
    cos, sin = jnp.cos(ang), jnp.sin(ang)
    n = pos.shape[0]
    rest = HEAD_DIM - ROT_DIM
    one = jnp.ones((n, rest), F32)
    zero = jnp.zeros((n, rest), F32)
    zh = jnp.zeros((n, half), F32)
    c = jnp.concatenate([cos, cos, one], axis=1)
    sa = jnp.concatenate([-sin, zh, zero], axis=1)
    sb = jnp.concatenate([zh, sin, zero], axis=1)
    rep = LANES // HEAD_DIM
    return tuple(jnp.tile(t, (1, rep)) for t in (c, sa, sb))


def _nsa_proj(x, w_pad, tables, n_tab_blocks):
    n, d = x.shape
    tm = tables[0].shape[0] // n_tab_blocks
    assert n % tm == 0
    row = lambda i: (i, 0)
    tab = pl.BlockSpec((tm, LANES), lambda i: (i % n_tab_blocks, 0))
    kv_spec = pl.BlockSpec((tm, KV_COLS), row)
    kv_shape = jax.ShapeDtypeStruct((n, KV_COLS), F32)
    return pl.pallas_call(
        _nsa_proj_kernel,
        grid=(n // tm,),
        in_specs=[pl.BlockSpec((tm, d), row), pl.BlockSpec(w_pad.shape, lambda i: (0, 0)), tab, tab, tab],
        out_specs=[pl.BlockSpec((tm, Q_COLS), row)] + [kv_spec] * 6 + [pl.BlockSpec((tm, LANES), row)],
        out_shape=[jax.ShapeDtypeStruct((n, Q_COLS), BF16)] + [kv_shape] * 6
                  + [jax.ShapeDtypeStruct((n, LANES), F32)],
        compiler_params=_cparams("parallel"),
    )(x, w_pad, *tables)


def _compress_kernel(roped, g_ref, pea_ref, peb_ref, wa_ref, wb_ref, w2_ref, cos_ref, sa_ref, sb_ref, o_ref):
    grp = g_ref[0]
    rows = grp.shape[0]
    a = _dot((grp + pea_ref[...]).astype(BF16), wa_ref[...])
    bm = _dot((grp + peb_ref[...]).astype(BF16), wb_ref[...])
    h = a + pltpu.roll(bm, rows - 1, 0)
    out = _dot(_silu(h).astype(BF16), w2_ref[...])
    if roped:
        cos, sa, sb = cos_ref[...], sa_ref[...], sb_ref[...]
        for j in range(KV_COLS // LANES):
            o_ref[0, :, j * LANES:(j + 1) * LANES] = _rope_chunk(out[:, j * LANES:(j + 1) * LANES], cos, sa, sb)
    else:
        o_ref[0] = out


def _compress(groups, pe, w1, w2, roped):
    bsz, ng, gw = groups.shape
    assert CMP_LEN == 2 * CMP_STRIDE and gw == CMP_STRIDE * KV_COLS
    pea, peb, wa, wb, w2b = _compress_weights(pe, w1, w2)
    c_end = jnp.arange(ng) * CMP_STRIDE + (CMP_LEN - 1)
    tables = _rope_tables(c_end)
    full = lambda b: (0, 0)
    tab = pl.BlockSpec((ng, LANES), full)
    return pl.pallas_call(
        functools.partial(_compress_kernel, roped),
        grid=(bsz,),
        in_specs=[pl.BlockSpec((1, ng, gw), lambda b: (b, 0, 0)),
                  pl.BlockSpec((1, gw), full), pl.BlockSpec((1, gw), full),
                  pl.BlockSpec(wa.shape, full), pl.BlockSpec(wb.shape, full), pl.BlockSpec(w2b.shape, full),
                  tab, tab, tab],
        out_specs=pl.BlockSpec((1, ng, KV_COLS), lambda b: (b, 0, 0)),
        out_shape=jax.ShapeDtypeStruct((bsz, ng, KV_COLS), F32),
        compiler_params=_cparams("parallel"),
    )(groups, pea, peb, wa, wb, w2b, *tables)


def _softmax_cols(s, mask, col_valid=None):
    s = jnp.where(mask, s, NEG)
    p = jnp.exp(s - jnp.max(s, axis=0, keepdims=True))
    r = 1.0 / jnp.maximum(jnp.sum(p, axis=0, keepdims=True), 1e-30)
    return p * (r if col_valid is None else jnp.where(col_valid, r, 0.0))


def _attn_kernel(n_cmp, n_blk, qT_ref, kc_ref, vcT_ref, oh_ref, ks_ref, vsT_ref, kw_ref, vwT_ref,
                 gate_ref, ovT_ref, o_ref):
    q0 = pl.program_id(2) * Q_TILE
    qT = qT_ref[0, 0, 0]
    cols = qT.shape[1]
    qpos = q0 + lax.broadcasted_iota(jnp.int32, (1, cols), 1) % Q_TILE
    nb_pad = ovT_ref.shape[0]

    kc = kc_ref[0, 0]
    ncp = kc.shape[0]
    cidx = lax.broadcasted_iota(jnp.int32, (ncp, 1), 0)
    c_end = jnp.where(cidx < n_cmp, cidx * CMP_STRIDE + (CMP_LEN - 1), jnp.iinfo(jnp.int32).max)
    p_c = _softmax_cols(_dot(kc, qT), c_end <= qpos, qpos >= CMP_LEN - 1)
    o_c = _dot(vcT_ref[0, 0], p_c.astype(BF16))

    p_sum = p_c[:, 0:Q_TILE]
    for g in range(1, GROUP):
        p_sum = p_sum + p_c[:, g * Q_TILE:(g + 1) * Q_TILE]
    p_hi, p_lo = _split_bf16(p_sum)
    imp = _dot(ovT_ref[...], p_hi) + _dot(ovT_ref[...], p_lo)

    blk = lax.broadcasted_iota(jnp.int32, (nb_pad, 1), 0)
    qp = qpos[:, 0:Q_TILE]
    lag = qp // SEL_BLOCK - blk
    forced = (blk == 0) | ((lag >= 0) & (lag < N_LOCAL_SEL))
    score = jnp.where(forced, BIG, jnp.where(blk * SEL_BLOCK <= qp, imp, -BIG))
    score = jnp.where(blk < n_blk, score, -jnp.inf)
    bias = jnp.full(score.shape, NEG, F32)
    for _ in range(min(N_SELECT, n_blk)):
        m = jnp.max(score, axis=0, keepdims=True)
        j = jnp.min(jnp.where(score == m, blk, nb_pad), axis=0, keepdims=True)
        hit = blk == j
        bias = jnp.where(hit, 0.0, bias)
        score = jnp.where(hit, -jnp.inf, score)
    bias = bias.astype(BF16)
    q_aug = jnp.concatenate([jnp.concatenate([bias] * GROUP, axis=1), qT], axis=0)

    def scores(kb):
        k0 = pl.multiple_of(kb * KEY_TILE, KEY_TILE)
        keys = jnp.concatenate([oh_ref[pl.ds(k0, KEY_TILE), :], ks_ref[0, 0, pl.ds(k0, KEY_TILE), :]], axis=1)
        kpos = k0 + lax.broadcasted_iota(jnp.int32, (KEY_TILE, 1), 0)
        return jnp.where(kpos <= qpos, _dot(keys, q_aug), NEG)

    def accumulate(kb, s, m, l, acc):
        k0 = pl.multiple_of(kb * KEY_TILE, KEY_TILE)
        m_new = jnp.maximum(m, jnp.max(s, axis=0, keepdims=True))
        alpha = jnp.exp(m - m_new)
        p = jnp.exp(s - m_new)
        l = alpha * l + jnp.sum(p, axis=0, keepdims=True)
        acc = alpha * acc + _dot(vsT_ref[0, 0, :, pl.ds(k0, KEY_TILE)], p.astype(BF16))
        return m_new, l, acc

    def body(kb, carry):
        s, m, l, acc = carry
        return (scores(kb + 1),) + accumulate(kb, s, m, l, acc)

    n_full = q0 // KEY_TILE
    init = (scores(0), jnp.full((1, cols), NEG, F32), jnp.zeros((1, cols), F32),
            jnp.zeros((HEAD_DIM, cols), F32))
    s_last, m_s, l_s, acc_s = lax.fori_loop(0, n_full, body, init)
    _, l_s, acc_s = accumulate(n_full, s_last, m_s, l_s, acc_s)
    o_s = acc_s * (1.0 / jnp.maximum(l_s, 1e-30))

    wlen = WINDOW + Q_TILE
    w0 = pl.multiple_of(jnp.maximum(q0 - WINDOW, 0), Q_TILE)
    wpos = w0 + lax.broadcasted_iota(jnp.int32, (wlen, 1), 0)
    in_window = jnp.abs((2 * qpos - (WINDOW - 1)) - 2 * wpos) <= WINDOW - 1
    p_w = _softmax_cols(_dot(kw_ref[0, 0, pl.ds(w0, wlen), :], qT), in_window)
    o_w = _dot(vwT_ref[0, 0, :, pl.ds(w0, wlen)], p_w.astype(BF16))

    gate = gate_ref[0, 0, 0]
    o_ref[0, 0, 0] = (gate[0:1] * o_c + gate[1:2] * o_s + gate[2:3] * o_w).astype(BF16)


def _overlap_t(n_cmp, n_cmp_pad, n_blk, nb_pad):
    cs = np.arange(n_cmp_pad)[None, :] * CMP_STRIDE
    ss = np.arange(nb_pad)[:, None] * SEL_BLOCK
    ov = np.maximum(np.minimum(cs + CMP_LEN, ss + SEL_BLOCK) - np.maximum(cs, ss), 0) / CMP_LEN
    ov = ov * (np.arange(n_cmp_pad)[None, :] < n_cmp) * (np.arange(nb_pad)[:, None] < n_blk)
    return jnp.asarray(ov, BF16)


def _nsa_attention(q, kc, vc, ks, vs, kw, vw, gates):
    bsz, t, _ = q.shape
    n_qt = t // Q_TILE
    n_cmp = (t - CMP_LEN) // CMP_STRIDE + 1
    n_blk = -(-t // SEL_BLOCK)
    nb_pad = _round_up(n_blk, 16)
    ncp = kc.shape[1]
    cols = GROUP * Q_TILE
    assert t % KEY_TILE == 0 and t >= WINDOW + Q_TILE and n_cmp <= ncp

    heads = lambda a: a.astype(BF16).reshape(bsz, a.shape[1], N_KV_HEADS, HEAD_DIM).transpose(0, 2, 1, 3)
    heads_t = lambda a: a.astype(BF16).reshape(bsz, a.shape[1], N_KV_HEADS, HEAD_DIM).transpose(0, 2, 3, 1)
    qT = q.reshape(bsz, n_qt, Q_TILE, N_KV_HEADS, GROUP, HEAD_DIM).transpose(0, 3, 1, 5, 4, 2)
    qT = qT.reshape(bsz, N_KV_HEADS, n_qt, HEAD_DIM, cols)
    onehot = jnp.asarray(np.arange(t)[:, None] // SEL_BLOCK == np.arange(nb_pad)[None, :], BF16)
    gT = gates[..., :3 * N_HEADS].reshape(bsz, n_qt, Q_TILE, N_KV_HEADS, GROUP, 3).transpose(0, 3, 1, 5, 4, 2)
    gT = jnp.pad(gT.reshape(bsz, N_KV_HEADS, n_qt, 3, cols), ((0, 0), (0, 0), (0, 0), (0, 5), (0, 0)))
    ovT = _overlap_t(n_cmp, ncp, n_blk, nb_pad)

    per_head = lambda *shape: pl.BlockSpec((1, 1) + shape, lambda b, h, i: (b, h, 0, 0))
    per_tile = lambda *shape: pl.BlockSpec((1, 1, 1) + shape, lambda b, h, i: (b, h, i, 0, 0))
    const = lambda a: pl.BlockSpec(a.shape, lambda b, h, i: (0, 0))
    oT = pl.pallas_call(
        functools.partial(_attn_kernel, n_cmp, n_blk),
        grid=(bsz, N_KV_HEADS, n_qt),
        in_specs=[per_tile(HEAD_DIM, cols),
                  per_head(ncp, HEAD_DIM), per_head(HEAD_DIM, ncp),
                  const(onehot), per_head(t, HEAD_DIM), per_head(HEAD_DIM, t),
                  per_head(t, HEAD_DIM), per_head(HEAD_DIM, t),
                  per_tile(8, cols), const(ovT)],
        out_specs=per_tile(HEAD_DIM, cols),
        out_shape=jax.ShapeDtypeStruct((bsz, N_KV_HEADS, n_qt, HEAD_DIM, cols), BF16),
        compiler_params=_cparams("parallel", "parallel", "arbitrary"),
    )(qT, heads(kc), heads_t(vc), onehot, heads(ks), heads_t(vs), heads(kw), heads_t(vw), gT, ovT)
    o = oT.reshape(bsz, N_KV_HEADS, n_qt, HEAD_DIM, GROUP, Q_TILE).transpose(0, 2, 5, 1, 4, 3)
    return o.reshape(bsz, t, Q_COLS)


def _paged_pipeline(pt_ref, pools, bufs, sems, pages_per_chunk, extra_pages, window):
    seq, chunk = pl.program_id(0), pl.program_id(1)
    n_chunk = pl.num_programs(1)
    step = seq * n_chunk + chunk
    total = pl.num_programs(0) * n_chunk
    slot = step % 2
    last_page = n_chunk * pages_per_chunk - 1
    n_copy = pages_per_chunk + extra_pages

    def start(sq, ck, sl):
        def body(j, carry):
            page = pt_ref[sq, jnp.minimum(ck * pages_per_chunk + j, last_page)]
            for i, (pool, buf) in enumerate(zip(pools, bufs)):
                pltpu.make_async_copy(pool.at[page], window(buf, sl, j), sems.at[i, sl]).start()
            return carry
        lax.fori_loop(0, n_copy, body, 0)

    @pl.when(step == 0)
    def _():
        start(seq, chunk, slot)

    @pl.when(step + 1 < total)
    def _():
        nxt = step + 1
        start(nxt // n_chunk, nxt % n_chunk, 1 - slot)

    def wait_body(j, carry):
        for i, (pool, buf) in enumerate(zip(pools, bufs)):
            pltpu.make_async_copy(pool.at[0], window(buf, slot, j), sems.at[i, slot]).wait()
        return carry
    lax.fori_loop(0, n_copy, wait_body, 0)
    return slot


def _compress_paged_kernel(layer, pages_per_chunk, n_cmp, pt_ref, poolk_ref, poolv_ref,
                           peak_ref, pebk_ref, wak_ref, wbk_ref, w2k_ref,
                           peav_ref, pebv_ref, wav_ref, wbv_ref, w2v_ref,
                           cos_ref, sa_ref, sb_ref, ok_ref, ov_ref, kbuf, vbuf, sems):
    gpp = poolk_ref.shape[2]
    rows = pages_per_chunk * gpp
    window = lambda buf, sl, j: buf.at[sl, pl.ds(pl.multiple_of(j * gpp, gpp), gpp)]
    slot = _paged_pipeline(pt_ref, (poolk_ref.at[layer], poolv_ref.at[layer]), (kbuf, vbuf), sems,
                           pages_per_chunk, 1, window)
    row0 = pl.program_id(1) * rows
    valid = row0 + lax.broadcasted_iota(jnp.int32, (rows, 1), 0) < n_cmp

    def mlp(buf, pea_ref, peb_ref, wa_ref, wb_ref, w2_ref):
        grp = buf[slot]
        a = _dot((grp[:rows] + pea_ref[...]).astype(BF16), wa_ref[...])
        bm = _dot((grp + peb_ref[...]).astype(BF16), wb_ref[...])
        h = a + pltpu.roll(bm, rows + gpp - 1, 0)[:rows]
        return jnp.where(valid, _dot(_silu(h).astype(BF16), w2_ref[...]), 0.0)

    outk = mlp(kbuf, peak_ref, pebk_ref, wak_ref, wbk_ref, w2k_ref)
    cos, sa, sb = cos_ref[...], sa_ref[...], sb_ref[...]
    for j in range(KV_COLS // LANES):
        ok_ref[0, :, j * LANES:(j + 1) * LANES] = _rope_chunk(
            outk[:, j * LANES:(j + 1) * LANES], cos, sa, sb).astype(BF16)
    ov_ref[0] = mlp(vbuf, peav_ref, pebv_ref, wav_ref, wbv_ref, w2v_ref).astype(BF16)


def _compress_weights(pe, w1, w2):
    eye = jnp.eye(N_KV_HEADS, dtype=F32)
    blockdiag = lambda w: jnp.einsum('lde,hg->lhdge', w, eye).reshape(-1, KV_COLS).astype(BF16)
    w2b = jnp.einsum('de,hg->hdge', w2, eye).reshape(KV_COLS, KV_COLS).astype(BF16)
    pe_row = lambda p: jnp.tile(p[:, None, :], (1, N_KV_HEADS, 1)).reshape(1, -1)
    return (pe_row(pe[:CMP_STRIDE]), pe_row(pe[CMP_STRIDE:]),
            blockdiag(w1[:CMP_STRIDE]), blockdiag(w1[CMP_STRIDE:]), w2b)


def _compress_paged(pool_k, pool_v, layer, page_table, pe, w1, w2, n_cmp):
    bd, n_pages = page_table.shape
    page = pool_k.shape[2]
    gpp = page // CMP_STRIDE
    gw = CMP_STRIDE * KV_COLS
    n_chunk = 2 if n_pages % 2 == 0 else 1
    ppc = n_pages // n_chunk
    rows = ppc * gpp
    ng = n_pages * gpp
    as_groups = lambda pool: pool.reshape(pool.shape[0], pool.shape[1], gpp, gw)
    wk = _compress_weights(pe[0], w1[0], w2[0])
    wv = _compress_weights(pe[1], w1[1], w2[1])
    tables = _rope_tables(jnp.arange(ng) * CMP_STRIDE + (CMP_LEN - 1))
    full = lambda b, c, pt: (0, 0)
    wspecs = [pl.BlockSpec(w.shape, full) for w in wk + wv]
    tab = pl.BlockSpec((rows, LANES), lambda b, c, pt: (c, 0))
    out_spec = pl.BlockSpec((1, rows, KV_COLS), lambda b, c, pt: (b, c, 0))
    out_shape = jax.ShapeDtypeStruct((bd, ng, KV_COLS), BF16)
    return pl.pallas_call(
        functools.partial(_compress_paged_kernel, layer, ppc, n_cmp),
        grid_spec=pltpu.PrefetchScalarGridSpec(
            num_scalar_prefetch=1, grid=(bd, n_chunk),
            in_specs=[pl.BlockSpec(memory_space=pl.ANY), pl.BlockSpec(memory_space=pl.ANY)] + wspecs + [tab] * 3,
            out_specs=[out_spec, out_spec],
            scratch_shapes=[pltpu.VMEM((2, rows + gpp, gw), F32), pltpu.VMEM((2, rows + gpp, gw), F32),
                            pltpu.SemaphoreType.DMA((2, 2))]),
        out_shape=[out_shape, out_shape],
        compiler_params=_cparams("arbitrary", "arbitrary"),
    )(page_table, as_groups(pool_k), as_groups(pool_v), *wk, *wv, *tables)


NEW_PAD = 128
CHUNK_PAGES = 16


def _dot_nt(a, b):
    return lax.dot_general(a, b, (((1,), (1,)), ((), ())), preferred_element_type=F32)


def _softmax_rows2(s1, mask1, s2, mask2):
    s1, s2 = jnp.where(mask1, s1, NEG), jnp.where(mask2, s2, NEG)
    m = jnp.maximum(jnp.max(s1, axis=-1, keepdims=True), jnp.max(s2, axis=-1, keepdims=True))
    p1 = jnp.where(mask1, jnp.exp(s1 - m), 0.0)
    p2 = jnp.where(mask2, jnp.exp(s2 - m), 0.0)
    r = 1.0 / jnp.maximum(jnp.sum(p1, axis=-1, keepdims=True) + jnp.sum(p2, axis=-1, keepdims=True), 1e-30)
    return p1 * r, p2 * r


def _attn_sample_kernel(layer, s_new, past, n_cmp, n_blk, pt_ref, poolk_ref, poolv_ref, q_ref, kc_ref, vc_ref,
                        knew_ref, vnew_ref, kwb_ref, vwb_ref, kwn_ref, vwn_ref, gate_ref,
                        et_ref, ett_ref, ov_ref, gsum_ref, o_ref,
                        kbuf, vbuf, sems, m_ref, l_ref, acc_ref, bias_ref, part_ref):
    page = poolk_ref.shape[-1]
    window = lambda buf, sl, j: buf.at[sl, :, :, pl.ds(pl.multiple_of(j * page, page), page)]
    slot = _paged_pipeline(pt_ref, (poolk_ref.at[layer], poolv_ref.at[layer]), (kbuf, vbuf), sems,
                           CHUNK_PAGES, 0, window)
    chunk = pl.program_id(1)
    q = q_ref[0]
    n_rows = q.shape[0]
    qpos = past + lax.broadcasted_iota(jnp.int32, (n_rows, 1), 0) % s_new
    gate = gate_ref[0]

    @pl.when(chunk == 0)
    def _():
        ncp = kc_ref.shape[1]
        cidx = lax.broadcasted_iota(jnp.int32, (1, ncp), 1)
        cmask = (cidx * CMP_STRIDE + (CMP_LEN - 1) <= qpos) & (cidx < n_cmp)
        s_c = jnp.where(cmask, _dot_nt(q, kc_ref[0]), NEG)
        p_c = jnp.where(cmask, jnp.exp(s_c - jnp.max(s_c, axis=-1, keepdims=True)), 0.0)
        p_c = p_c * (1.0 / jnp.maximum(jnp.sum(p_c, axis=-1, keepdims=True), 1e-30))
        o_c = _dot(p_c.astype(BF16), vc_ref[0])
        p_hi, p_lo = _split_bf16(p_c)
        p_sum = _dot(gsum_ref[...], p_hi) + _dot(gsum_ref[...], p_lo)
        s_hi, s_lo = _split_bf16(p_sum)
        imp = _dot(s_hi, ov_ref[...]) + _dot(s_lo, ov_ref[...])
        nb_pad = imp.shape[1]
        blk = lax.broadcasted_iota(jnp.int32, (1, nb_pad), 1)
        lag = qpos // SEL_BLOCK - blk
        forced = (blk == 0) | ((lag >= 0) & (lag < N_LOCAL_SEL))
        score = jnp.where(forced, BIG, jnp.where(blk * SEL_BLOCK <= qpos, imp, -BIG))
        score = jnp.where(blk < n_blk, score, -jnp.inf)
        bias = jnp.full(score.shape, NEG, F32)
        for _ in range(min(N_SELECT, n_blk)):
            m = jnp.max(score, axis=-1, keepdims=True)
            j = jnp.min(jnp.where(score == m, blk, nb_pad), axis=-1, keepdims=True)
            hit = blk == j
            bias = jnp.where(hit, 0.0, bias)
            score = jnp.where(hit, -jnp.inf, score)
        bias_ref[...] = bias
        w_buf = kwb_ref.shape[1]
        wpos = past - w_buf + lax.broadcasted_iota(jnp.int32, (1, w_buf), 1)
        npos = lax.broadcasted_iota(jnp.int32, (1, NEW_PAD), 1)
        p1, p2 = _softmax_rows2(
            _dot_nt(q, kwb_ref[0].astype(BF16)), (wpos <= qpos) & (wpos > qpos - WINDOW),
            _dot_nt(q, kwn_ref[0]), (past + npos <= qpos) & (past + npos > qpos - WINDOW) & (npos < s_new))
        o_w = _dot(p1.astype(BF16), vwb_ref[0].astype(BF16)) + _dot(p2.astype(BF16), vwn_ref[0])
        part_ref[...] = gate[:, 0:1] * o_c + gate[:, 2:3] * o_w
        m_ref[...] = jnp.full(m_ref.shape, NEG, F32)
        l_ref[...] = jnp.zeros(l_ref.shape, F32)
        acc_ref[...] = jnp.zeros(acc_ref.shape, F32)

    bias16 = bias_ref[...].astype(BF16)

    def flash(s, pv):
        m_new = jnp.maximum(m_ref[...], jnp.max(s, axis=-1, keepdims=True))
        alpha = jnp.exp(m_ref[...] - m_new)
        p = jnp.exp(s - m_new)
        l_ref[...] = alpha * l_ref[...] + jnp.sum(p, axis=-1, keepdims=True)
        acc_ref[...] = alpha * acc_ref[...] + pv(p.astype(BF16))
        m_ref[...] = m_new

    rph = n_rows // N_KV_HEADS
    own = lambda a, h: a[h * rph:(h + 1) * rph, h * HEAD_DIM:(h + 1) * HEAD_DIM]

    def pv_chunk(p):
        rows = []
        for h in range(N_KV_HEADS):
            o_h = _dot_nt(p[h * rph:(h + 1) * rph], vbuf[slot, h].astype(BF16))
            zero = jnp.zeros((rph, HEAD_DIM), F32)
            rows.append(jnp.concatenate([zero] * h + [o_h] + [zero] * (N_KV_HEADS - 1 - h), axis=1))
        return jnp.concatenate(rows, axis=0)

    s_past = jnp.concatenate([_dot(own(q, h), kbuf[slot, h].astype(BF16)) for h in range(N_KV_HEADS)], axis=0)
    flash(s_past + _dot(bias16, et_ref[...]), pv_chunk)

    @pl.when(chunk == pl.num_programs(1) - 1)
    def _():
        npos = lax.broadcasted_iota(jnp.int32, (1, NEW_PAD), 1)
        s_t = _dot_nt(q, knew_ref[0]) + _dot(bias16, ett_ref[...])
        flash(jnp.where((past + npos <= qpos) & (npos < s_new), s_t, NEG), lambda p: _dot(p, vnew_ref[0]))
        o_s = acc_ref[...] * (1.0 / jnp.maximum(l_ref[...], 1e-30))
        o_ref[0] = part_ref[...] + gate[:, 1:2] * o_s


def _attn_sample(qs, kc, vc, pool_k, pool_v, layer, page_table, k_new, v_new, kw_buf, vw_buf, kw_new, vw_new,
                 gates, *, s_new):
    bd, n_pages = page_table.shape
    page = pool_k.shape[2]
    past = n_pages * page
    w_buf = kw_buf.shape[1]
    n_rows = N_HEADS * s_new
    per_head_pages = lambda pool: pool.transpose(0, 1, 3, 4, 2)
    n_cmp = (past + s_new - CMP_LEN) // CMP_STRIDE + 1
    n_blk = -(-(past + s_new) // SEL_BLOCK)
    nb_pad = _round_up(n_blk, 2 * LANES)
    ncp = kc.shape[1]
    assert n_pages % CHUNK_PAGES == 0 and s_new <= NEW_PAD and n_cmp <= ncp and w_buf == min(WINDOW, past)
    ck = CHUNK_PAGES * page
    eye = jnp.eye(N_KV_HEADS, dtype=BF16)

    q5 = qs.reshape(bd, s_new, N_KV_HEADS, GROUP, HEAD_DIM).transpose(0, 2, 3, 1, 4)
    qbig = jnp.einsum('bkgsd,kj->bkgsjd', q5, eye).reshape(bd, n_rows, KV_COLS)
    g5 = gates[:, :3 * N_HEADS].reshape(bd, s_new, N_KV_HEADS, GROUP, 3).transpose(0, 2, 3, 1, 4)
    gate = jnp.pad(g5.reshape(bd, n_rows, 3), ((0, 0), (0, 0), (0, LANES - 3)))
    newrows = lambda a: jnp.pad(a.reshape(bd, s_new, KV_COLS), ((0, 0), (0, NEW_PAD - s_new), (0, 0))).astype(BF16)
    flat = lambda a: a.reshape(a.shape[0], a.shape[1], KV_COLS)

    keyblk = np.arange(past + NEW_PAD)[None, :] // SEL_BLOCK
    et = jnp.asarray(keyblk == np.arange(nb_pad)[:, None], BF16)
    cs = np.arange(ncp)[:, None] * CMP_STRIDE
    ss = np.arange(nb_pad)[None, :] * SEL_BLOCK
    ov = np.maximum(np.minimum(cs + CMP_LEN, ss + SEL_BLOCK) - np.maximum(cs, ss), 0) / CMP_LEN
    ov = jnp.asarray(ov * (np.arange(ncp)[:, None] < n_cmp) * (np.arange(nb_pad)[None, :] < n_blk), BF16)
    r = np.arange(n_rows)
    same = (r[:, None] // (GROUP * s_new) == r[None, :] // (GROUP * s_new)) & (r[:, None] % s_new == r[None, :] % s_new)
    gsum = jnp.asarray(same, BF16)

    per_seq = lambda a: pl.BlockSpec((1,) + a.shape[1:], lambda b, c, pt: (b, 0, 0))
    const = lambda a: pl.BlockSpec(a.shape, lambda b, c, pt: (0, 0))
    operands = [qbig, kc, vc, newrows(k_new), newrows(v_new), flat(kw_buf), flat(vw_buf),
                newrows(kw_new), newrows(vw_new), gate]
    o_big = pl.pallas_call(
        functools.partial(_attn_sample_kernel, layer, s_new, past, n_cmp, n_blk),
        grid_spec=pltpu.PrefetchScalarGridSpec(
            num_scalar_prefetch=1, grid=(bd, n_pages // CHUNK_PAGES),
            in_specs=[pl.BlockSpec(memory_space=pl.ANY), pl.BlockSpec(memory_space=pl.ANY)]
                     + [per_seq(a) for a in operands]
                     + [pl.BlockSpec((nb_pad, ck), lambda b, c, pt: (0, c)),
                        pl.BlockSpec((nb_pad, NEW_PAD), lambda b, c, pt: (0, past // NEW_PAD)),
                        const(ov), const(gsum)],
            out_specs=pl.BlockSpec((1, n_rows, KV_COLS), lambda b, c, pt: (b, 0, 0)),
            scratch_shapes=[pltpu.VMEM((2, N_KV_HEADS, HEAD_DIM, ck), F32),
                            pltpu.VMEM((2, N_KV_HEADS, HEAD_DIM, ck), F32),
                            pltpu.SemaphoreType.DMA((2, 2)),
                            pltpu.VMEM((n_rows, 1), F32), pltpu.VMEM((n_rows, 1), F32),
                            pltpu.VMEM((n_rows, KV_COLS), F32), pltpu.VMEM((n_rows, nb_pad), F32),
                            pltpu.VMEM((n_rows, KV_COLS), F32)]),
        out_shape=jax.ShapeDtypeStruct((bd, n_rows, KV_COLS), F32),
        compiler_params=_cparams("arbitrary", "arbitrary"),
    )(page_table, per_head_pages(pool_k), per_head_pages(pool_v), *operands, et, et, ov, gsum)
    o6 = o_big.reshape(bd, N_KV_HEADS, GROUP, s_new, N_KV_HEADS, HEAD_DIM)
    own = jnp.eye(N_KV_HEADS, dtype=F32)[None, :, None, None, :, None]
    o = jnp.sum(o6 * own, axis=4).transpose(0, 3, 1, 2, 4)
    return o.reshape(bd * s_new, Q_COLS).astype(BF16)


def _proj_ln_kernel(alpha, a_ref, w_ref, x_ref, g_ref, b_ref, o_ref):
    h = _dot(a_ref[...], w_ref[...])
    o_ref[...] = _layer_norm(alpha * x_ref[...] + h, g_ref[...], b_ref[...])


def _proj_ln(a, w, x, g, b, alpha):
    n, d = x.shape
    tm = _row_tile(n, 512)
    vec = pl.BlockSpec((1, d), lambda i: (0, 0))
    return pl.pallas_call(
        functools.partial(_proj_ln_kernel, alpha),
        grid=(n // tm,),
        in_specs=[pl.BlockSpec((tm, a.shape[1]), lambda i: (i, 0)), pl.BlockSpec(w.shape, lambda i: (0, 0)),
                  pl.BlockSpec((tm, d), lambda i: (i, 0)), vec, vec],
        out_specs=pl.BlockSpec((tm, d), lambda i: (i, 0)),
        out_shape=jax.ShapeDtypeStruct((n, d), F32),
        compiler_params=_cparams("parallel"),
    )(a, w, x, g, b)


def _conv_layer(yp, ys, state, w_pw1, b_pw1, w_dw, b_dw, cg, cb, w_pw2, b_pw2, g, b, alpha):
    bsz, t, d = yp.shape
    bd, s, _ = ys.shape
    width = w_dw.shape[0]
    wa, wg = w_pw1[:, :d].astype(BF16), w_pw1[:, d:].astype(BF16)
    ba, bg = b_pw1[None, :d], b_pw1[None, d:]
    w2 = w_pw2.astype(BF16)
    row = lambda v: v[None, :]
    tail = (w_dw, row(b_dw), row(cg), row(cb), w2, row(b_pw2), row(g), row(b), alpha)

    u_p = _pw1_glu(yp.reshape(bsz * t, d), wa, wg, ba, bg).reshape(bsz, t, d)
    new_p = _conv2_prompt(u_p, yp, *tail)
    st_p = u_p[:, t - (width - 1):]

    u_s = _pw1_glu(ys.reshape(bd * s, d), wa, wg, ba, bg).reshape(bd, s, d)
    ext = jnp.concatenate([state, u_s], axis=1)
    rows = _round_up(s, 8)
    ext_pad = jnp.pad(ext, ((0, 0), (0, _round_up(width - 1 + rows, 8) - ext.shape[1]), (0, 0)))
    xs_pad = jnp.pad(ys, ((0, 0), (0, rows - s), (0, 0)))
    new_s = _conv2_sample(ext_pad, xs_pad, *tail)[:, :s]
    st_s = ext[:, s:]
    return new_p, new_s, st_p, st_s


def _nsa_layer(yp, ys, pool_kc, pool_vc, pool_ks, pool_vs, layer, kw_buf, vw_buf, page_table,
               w_in, pe, w1, w2, w_o, g, b, alpha):
    bsz, t, d = yp.shape
    bd, s, _ = ys.shape
    page = pool_kc.shape[2]
    past = page_table.shape[1] * page
    w_buf = kw_buf.shape[1]
    in_cols = w_in.shape[1]
    w_pad = jnp.pad(w_in, ((0, 0), (0, Q_COLS + 6 * KV_COLS + LANES - in_cols))).astype(BF16)
    w_o16 = w_o.astype(BF16)
    g, b = g[None, :], b[None, :]
    kv4 = lambda a, nb, nt: a.reshape(nb, nt, N_KV_HEADS, HEAD_DIM)

    tm = _row_tile(t, 256)
    tabs = _rope_tables(jnp.arange(t))
    q, k_c, v_c, k_s, v_s, k_w, v_w, gates = _nsa_proj(yp.reshape(bsz * t, d), w_pad, tabs, t // tm)
    b3 = lambda a: a.reshape(bsz, t, -1)
    grp = lambda a: a.reshape(bsz, t // CMP_STRIDE, CMP_STRIDE * KV_COLS)
    kc = _compress(grp(k_c), pe[0], w1[0], w2[0], True)
    vc = _compress(grp(v_c), pe[1], w1[1], w2[1], False)
    o = _nsa_attention(b3(q), kc, vc, b3(k_s), b3(v_s), b3(k_w), b3(v_w), b3(gates))
    new_p = _proj_ln(o.reshape(bsz * t, Q_COLS), w_o16, yp.reshape(bsz * t, d), g, b, alpha).reshape(bsz, t, d)
    win = lambda a: jnp.concatenate([jnp.zeros((bsz, w_buf, KV_COLS), F32), b3(a)], axis=1)[:, -w_buf:]
    st_p = tuple(kv4(a, bsz, t) for a in (k_c, v_c, k_s, v_s)) + (
        kv4(win(k_w), bsz, w_buf), kv4(win(v_w), bsz, w_buf))

    n_s = bd * s
    tabs_s = tuple(jnp.tile(tt, (bd, 1)) for tt in _rope_tables(past + jnp.arange(s)))
    qs, k_c2, v_c2, k_s2, v_s2, k_w2, v_w2, gates_s = _nsa_proj(ys.reshape(n_s, d), w_pad, tabs_s, 1)
    s3 = lambda a: a.reshape(bd, s, -1)
    n_cmp_s = (past + s - CMP_LEN) // CMP_STRIDE + 1
    assert n_cmp_s <= past // CMP_STRIDE
    kc_s, vc_s = _compress_paged(pool_kc, pool_vc, layer, page_table, pe, w1, w2, n_cmp_s)
    o_s = _attn_sample(qs, kc_s, vc_s, pool_ks, pool_vs, layer, page_table, k_s2, v_s2, kw_buf, vw_buf, k_w2, v_w2,
                       gates_s, s_new=s)
    kw_all = jnp.concatenate([kw_buf.reshape(bd, w_buf, KV_COLS), s3(k_w2)], axis=1)
    vw_all = jnp.concatenate([vw_buf.reshape(bd, w_buf, KV_COLS), s3(v_w2)], axis=1)
    new_s = _proj_ln(o_s, w_o16, ys.reshape(n_s, d), g, b, alpha).reshape(bd, s, d)
    st_s = tuple(kv4(a, bd, s) for a in (k_c2, v_c2, k_s2, v_s2)) + (
        kv4(kw_all[:, -w_buf:], bd, w_buf), kv4(vw_all[:, -w_buf:], bd, w_buf))
    return new_p, new_s, st_p, st_s


def kernel(x_prompt, x_sample, state_conv, cache_k_cmp, cache_v_cmp, cache_k_sel, cache_v_sel, state_k_win, state_v_win, page_table, ln_g, ln_b, conv_w_pw1, conv_b_pw1, conv_w_dw, conv_b_dw, conv_ln_g, conv_ln_b, conv_w_pw2, conv_b_pw2, nsa_w_in, nsa_cmp_pe, nsa_cmp_w1, nsa_cmp_w2, nsa_w_o, ffn_w_gate, ffn_w_up, ffn_w_down, moe_w_router, moe_w_gate, moe_w_up, moe_w_down):
    depth = ln_g.shape[0]
    alpha = (2 * depth) ** 0.25
    bsz, t, d = x_prompt.shape
    bd, s, _ = x_sample.shape
    yp, ys = x_prompt, x_sample
    conv_p, conv_s, nsa_p, nsa_s = [], [], [], []
    for i in range(depth):
        j = i // 2
        if i % 2 == 0:
            yp, ys, st_p, st_s = _conv_layer(
                yp, ys, state_conv[j], conv_w_pw1[j], conv_b_pw1[j], conv_w_dw[j], conv_b_dw[j],
                conv_ln_g[j], conv_ln_b[j], conv_w_pw2[j], conv_b_pw2[j], ln_g[i, 0], ln_b[i, 0], alpha)
            conv_p.append(st_p)
            conv_s.append(st_s)
        else:
            yp, ys, st_p, st_s = _nsa_layer(
                yp, ys, cache_k_cmp, cache_v_cmp, cache_k_sel, cache_v_sel, j,
                state_k_win[j], state_v_win[j], page_table,
                nsa_w_in[j], nsa_cmp_pe[j], nsa_cmp_w1[j], nsa_cmp_w2[j], nsa_w_o[j],
                ln_g[i, 0], ln_b[i, 0], alpha)
            nsa_p.append(st_p)
            nsa_s.append(st_s)
        g2, b2 = ln_g[i, 1][None, :], ln_b[i, 1][None, :]
        xp, xs = yp.reshape(bsz * t, d), ys.reshape(bd * s, d)
        if i % 2 == 0:
            fw = (ffn_w_gate[j].astype(BF16), ffn_w_up[j].astype(BF16), ffn_w_down[j].astype(BF16))
            xp = _ffn_ln(xp, *fw, g2, b2, alpha)
            xs = _ffn_ln(xs, *fw, g2, b2, alpha)
        else:
            mw = (moe_w_gate[j].astype(BF16), moe_w_up[j].astype(BF16), moe_w_down[j].astype(BF16))
            both = _moe_ln(jnp.concatenate([xp, xs], axis=0), moe_w_router[j], *mw, g2, b2, alpha)
            xp, xs = both[:bsz * t], both[bsz * t:]
        yp, ys = xp.reshape(bsz, t, d), xs.reshape(bd, s, d)
    stack = lambda parts: tuple(jnp.stack(a) for a in zip(*parts))
    return ((yp, ys, jnp.stack(conv_p)) + stack(nsa_p) + (jnp.stack(conv_s),) + stack(nsa_s))
```

```python
import functools

import numpy as np
import jax
import jax.numpy as jnp
from jax import lax
from jax.experimental import pallas as pl
from jax.experimental.pallas import tpu as pltpu

F32 = jnp.float32
BF16 = jnp.bfloat16

N_HEADS = 16
N_KV_HEADS = 4
GROUP = N_HEADS // N_KV_HEADS
HEAD_DIM = 64
ROT_DIM = HEAD_DIM // 4
ROPE_THETA = 500000.0
CMP_LEN = 32
CMP_STRIDE = 16
SEL_BLOCK = 64
N_SELECT = 16
N_LOCAL_SEL = 2
WINDOW = 512
TOP_K = 2
LN_EPS = 1e-5
NEG = -1e30
BIG = 1e30

Q_TILE = 128
KEY_TILE = 512
KV_COLS = N_KV_HEADS * HEAD_DIM
Q_COLS = N_HEADS * HEAD_DIM
LANES = 128
VMEM_LIMIT = 56 * 1024 * 1024


def _cparams(*sem):
    return pltpu.CompilerParams(dimension_semantics=sem, vmem_limit_bytes=VMEM_LIMIT)


def _round_up(a, m):
    return -(-a // m) * m


def _row_tile(n, pref):
    t = min(n, pref)
    while n % t:
        t //= 2
    assert t >= 8 and n % t == 0
    return t


def _sigmoid(v):
    return 1.0 / (1.0 + jnp.exp(-v))


def _silu(v):
    return v * _sigmoid(v)


def _layer_norm(v, g, b):
    mu = jnp.mean(v, axis=-1, keepdims=True)
    d = v - mu
    var = jnp.mean(d * d, axis=-1, keepdims=True)
    return d * lax.rsqrt(var + LN_EPS) * g + b


def _dot(a, b):
    return jnp.dot(a, b, preferred_element_type=F32)


def _split_bf16(v):
    hi = v.astype(BF16)
    lo = (v - hi.astype(F32)).astype(BF16)
    return hi, lo


def _pw1_glu_kernel(x_ref, wa_ref, wg_ref, ba_ref, bg_ref, u_ref):
    x = x_ref[...].astype(BF16)
    a = _dot(x, wa_ref[...]) + ba_ref[...]
    g = _dot(x, wg_ref[...]) + bg_ref[...]
    u_ref[...] = a * _sigmoid(g)


def _pw1_glu(x, wa, wg, ba, bg):
    n, d = x.shape
    tm = _row_tile(n, 512)
    full = lambda i: (0, 0)
    return pl.pallas_call(
        _pw1_glu_kernel,
        grid=(n // tm,),
        in_specs=[pl.BlockSpec((tm, d), lambda i: (i, 0)),
                  pl.BlockSpec(wa.shape, full), pl.BlockSpec(wg.shape, full),
                  pl.BlockSpec(ba.shape, full), pl.BlockSpec(bg.shape, full)],
        out_specs=pl.BlockSpec((tm, d), lambda i: (i, 0)),
        out_shape=jax.ShapeDtypeStruct((n, d), F32),
        compiler_params=_cparams("parallel"),
    )(x, wa, wg, ba, bg)


HALO = 32


def _dw_taps(ext_ref, wdw_ref, rows, width):
    off = HALO - (width - 1)
    acc = wdw_ref[0:1, :] * ext_ref[pl.ds(off, rows), :]
    for k in range(1, width):
        acc = acc + wdw_ref[k:k + 1, :] * ext_ref[pl.ds(off + k, rows), :]
    return acc


def _conv2_prompt_kernel(width, alpha, u_ref, halo_ref, x_ref, wdw_ref, bdw_ref, cg_ref, cb_ref,
                         w2_ref, b2_ref, g_ref, b_ref, o_ref, ext_ref):
    i = pl.program_id(1)
    tm = u_ref.shape[1]

    @pl.when(i == 0)
    def _():
        ext_ref[0:HALO, :] = jnp.zeros((HALO, ext_ref.shape[1]), F32)

    @pl.when(i > 0)
    def _():
        ext_ref[0:HALO, :] = halo_ref[0]

    ext_ref[HALO:, :] = u_ref[0]
    c = _dw_taps(ext_ref, wdw_ref, tm, width) + bdw_ref[...]
    c = _layer_norm(c, cg_ref[...], cb_ref[...])
    h = _dot(_silu(c).astype(BF16), w2_ref[...]) + b2_ref[...]
    o_ref[0] = _layer_norm(alpha * x_ref[0] + h, g_ref[...], b_ref[...])


def _conv2_prompt(u, x, wdw, bdw, cg, cb, w2, b2, g, b, alpha):
    bsz, t, d = u.shape
    width = wdw.shape[0]
    tm = _row_tile(t, 256)
    assert tm % HALO == 0 and width - 1 <= HALO
    r = tm // HALO
    full = lambda bi, i: (0, 0)
    vec = pl.BlockSpec((1, d), full)
    return pl.pallas_call(
        functools.partial(_conv2_prompt_kernel, width, alpha),
        grid=(bsz, t // tm),
        in_specs=[pl.BlockSpec((1, tm, d), lambda bi, i: (bi, i, 0)),
                  pl.BlockSpec((1, HALO, d), lambda bi, i: (bi, jnp.maximum(i * r - 1, 0), 0)),
                  pl.BlockSpec((1, tm, d), lambda bi, i: (bi, i, 0)),
                  pl.BlockSpec(wdw.shape, full), vec, vec, vec,
                  pl.BlockSpec(w2.shape, full), vec, vec, vec],
        out_specs=pl.BlockSpec((1, tm, d), lambda bi, i: (bi, i, 0)),
        out_shape=jax.ShapeDtypeStruct((bsz, t, d), F32),
        scratch_shapes=[pltpu.VMEM((HALO + tm, d), F32)],
        compiler_params=_cparams("parallel", "arbitrary"),
    )(u, u, x, wdw, bdw, cg, cb, w2, b2, g, b)


def _conv2_sample_kernel(width, alpha, ext_ref, x_ref, wdw_ref, bdw_ref, cg_ref, cb_ref,
                         w2_ref, b2_ref, g_ref, b_ref, o_ref):
    bsz, rows, d = x_ref.shape
    acc = wdw_ref[0:1, :][None] * ext_ref[:, pl.ds(0, rows), :]
    for k in range(1, width):
        acc = acc + wdw_ref[k:k + 1, :][None] * ext_ref[:, pl.ds(k, rows), :]
    c = acc.reshape(bsz * rows, d) + bdw_ref[...]
    c = _layer_norm(c, cg_ref[...], cb_ref[...])
    h = _dot(_silu(c).astype(BF16), w2_ref[...]) + b2_ref[...]
    x = x_ref[...].reshape(bsz * rows, d)
    o_ref[...] = _layer_norm(alpha * x + h, g_ref[...], b_ref[...]).reshape(bsz, rows, d)


def _conv2_sample(ext, x, wdw, bdw, cg, cb, w2, b2, g, b, alpha):
    bsz, rows, d = x.shape
    return pl.pallas_call(
        functools.partial(_conv2_sample_kernel, wdw.shape[0], alpha),
        out_shape=jax.ShapeDtypeStruct((bsz, rows, d), F32),
        compiler_params=pltpu.CompilerParams(vmem_limit_bytes=VMEM_LIMIT),
    )(ext, x, wdw, bdw, cg, cb, w2, b2, g, b)


def _ffn_kernel(alpha, x_ref, wg_ref, wu_ref, wd_ref, g_ref, b_ref, o_ref, acc_ref):
    k = pl.program_id(1)

    @pl.when(k == 0)
    def _():
        acc_ref[...] = jnp.zeros(acc_ref.shape, F32)

    x = x_ref[...].astype(BF16)
    h = _silu(_dot(x, wg_ref[...])) * _dot(x, wu_ref[...])
    acc_ref[...] += _dot(h.astype(BF16), wd_ref[...])

    @pl.when(k == pl.num_programs(1) - 1)
    def _():
        o_ref[...] = _layer_norm(alpha * x_ref[...] + acc_ref[...], g_ref[...], b_ref[...])


def _ff_tile(dff):
    for tf in (512, 384, 256, 128):
        if dff % tf == 0:
            return tf
    return dff


def _ffn_ln(x, wg, wu, wd, g, b, alpha):
    n, d = x.shape
    dff = wg.shape[1]
    tm = _row_tile(n, 1024)
    tf = _ff_tile(dff)
    vec = pl.BlockSpec((1, d), lambda i, k: (0, 0))
    return pl.pallas_call(
        functools.partial(_ffn_kernel, alpha),
        grid=(n // tm, dff // tf),
        in_specs=[pl.BlockSpec((tm, d), lambda i, k: (i, 0)),
                  pl.BlockSpec((d, tf), lambda i, k: (0, k)),
                  pl.BlockSpec((d, tf), lambda i, k: (0, k)),
                  pl.BlockSpec((tf, d), lambda i, k: (k, 0)), vec, vec],
        out_specs=pl.BlockSpec((tm, d), lambda i, k: (i, 0)),
        out_shape=jax.ShapeDtypeStruct((n, d), F32),
        scratch_shapes=[pltpu.VMEM((tm, d), F32)],
        compiler_params=_cparams("parallel", "arbitrary"),
    )(x, wg, wu, wd, g, b)


MOE_TILE = 512
MOE_CHUNK = 128
MOE_ROWS = 512
SEG_ALIGN = 16
N_CHUNK = MOE_TILE // MOE_CHUNK


def _router_kernel(n_exp, n_valid, x_ref, wh_ref, wl_ref, comb_ref, cnt_ref):
    xh, xl = _split_bf16(x_ref[...])
    logits = _dot(xh, wh_ref[...]) + (_dot(xh, wl_ref[...]) + _dot(xl, wh_ref[...]))
    lane = lax.broadcasted_iota(jnp.int32, logits.shape, 1)
    logits = jnp.where(lane < n_exp, logits, -jnp.inf)
    m1 = jnp.max(logits, axis=-1, keepdims=True)
    i1 = jnp.min(jnp.where(logits == m1, lane, LANES), axis=-1, keepdims=True)
    rest = jnp.where(lane == i1, -jnp.inf, logits)
    m2 = jnp.max(rest, axis=-1, keepdims=True)
    i2 = jnp.min(jnp.where(rest == m2, lane, LANES), axis=-1, keepdims=True)
    e2 = jnp.exp(m2 - m1)
    w1 = 1.0 / (1.0 + e2)
    w2 = e2 / (1.0 + e2)
    row = pl.program_id(0) * logits.shape[0] + lax.broadcasted_iota(jnp.int32, (logits.shape[0], 1), 0)
    comb = jnp.where(row < n_valid, jnp.where(lane == i1, w1, 0.0) + jnp.where(lane == i2, w2, 0.0), 0.0)
    comb_ref[...] = comb
    cnt_ref[0] = jnp.sum(jnp.where(comb > 0.0, 1.0, 0.0), axis=0, keepdims=True)


def _router(x, w_router, n_valid):
    n, d = x.shape
    n_exp = w_router.shape[1]
    assert TOP_K == 2 and n_exp <= LANES and n % MOE_TILE == 0
    wpad = jnp.pad(w_router, ((0, 0), (0, LANES - n_exp)))
    wh = wpad.astype(BF16)
    wl = (wpad - wh.astype(F32)).astype(BF16)
    n_tiles = n // MOE_TILE
    return pl.pallas_call(
        functools.partial(_router_kernel, n_exp, n_valid),
        grid=(n_tiles,),
        in_specs=[pl.BlockSpec((MOE_TILE, d), lambda i: (i, 0)),
                  pl.BlockSpec((d, LANES), lambda i: (0, 0)),
                  pl.BlockSpec((d, LANES), lambda i: (0, 0))],
        out_specs=[pl.BlockSpec((MOE_TILE, LANES), lambda i: (i, 0)),
                   pl.BlockSpec((1, 1, LANES), lambda i: (i, 0, 0))],
        out_shape=[jax.ShapeDtypeStruct((n, LANES), F32), jax.ShapeDtypeStruct((n_tiles, 1, LANES), F32)],
        compiler_params=_cparams("parallel"),
    )(x, wh, wl)


def _segment_copy(hbm_ref, buf, sems, base_ref, tile, n_exp, e, ch, to_hbm):
    off = pl.multiple_of(base_ref[tile * n_exp + e] + ch * MOE_CHUNK, SEG_ALIGN)
    slot = e * N_CHUNK + ch
    rows = hbm_ref.at[pl.ds(off, MOE_CHUNK)]
    src, dst = (buf.at[slot], rows) if to_hbm else (rows, buf.at[slot])
    return pltpu.make_async_copy(src, dst, sems.at[slot])


def _dispatch_kernel(n_exp, base_ref, cnt_ref, x_ref, comb_ref, upper_ref, zeros_ref, xs_ref, buf, sems):
    del zeros_ref
    tile = pl.program_id(0)
    x16 = x_ref[...].astype(BF16)
    sel_t = jnp.where(comb_ref[...] > 0.0, 1.0, 0.0).T
    rank_t = _dot(sel_t.astype(BF16), upper_ref[...])
    used = lambda e, ch: ch * MOE_CHUNK < cnt_ref[tile * n_exp + e]
    for e in range(n_exp):
        for ch in range(N_CHUNK):
            @pl.when(used(e, ch))
            def _(e=e, ch=ch):
                rows = (ch * MOE_CHUNK + lax.broadcasted_iota(jnp.int32, (MOE_CHUNK, 1), 0)).astype(F32)
                pick = jnp.where(rank_t[e:e + 1] == rows, sel_t[e:e + 1], 0.0).astype(BF16)
                buf[e * N_CHUNK + ch] = _dot(pick, x16).astype(BF16)
                _segment_copy(xs_ref, buf, sems, base_ref, tile, n_exp, e, ch, True).start()
    for e in range(n_exp):
        for ch in range(N_CHUNK):
            @pl.when(used(e, ch))
            def _(e=e, ch=ch):
                _segment_copy(xs_ref, buf, sems, base_ref, tile, n_exp, e, ch, True).wait()


def _expert_ffn_kernel(te_ref, tv_ref, x_ref, wg_ref, wu_ref, wd_ref, o_ref, acc_ref):
    del te_ref
    j, k = pl.program_id(0), pl.program_id(1)

    @pl.when(k == 0)
    def _():
        acc_ref[...] = jnp.zeros(acc_ref.shape, F32)

    @pl.when(tv_ref[j] > 0)
    def _():
        x = x_ref[...]
        h = _silu(_dot(x, wg_ref[0])) * _dot(x, wu_ref[0])
        acc_ref[...] += _dot(h.astype(BF16), wd_ref[0])

    @pl.when(k == pl.num_programs(1) - 1)
    def _():
        o_ref[...] = acc_ref[...].astype(BF16)


def _combine_kernel(n_exp, alpha, base_ref, cnt_ref, x_ref, comb_ref, lower_ref, ys_ref, g_ref, b_ref,
                    o_ref, buf, sems, acc_ref):
    tile = pl.program_id(0)
    used = lambda e, ch: ch * MOE_CHUNK < cnt_ref[tile * n_exp + e]
    for e in range(n_exp):
        for ch in range(N_CHUNK):
            @pl.when(used(e, ch))
            def _(e=e, ch=ch):
                _segment_copy(ys_ref, buf, sems, base_ref, tile, n_exp, e, ch, False).start()
    comb = comb_ref[...]
    sel = jnp.where(comb > 0.0, 1.0, 0.0)
    rank = _dot(lower_ref[...], sel.astype(BF16))
    acc_ref[...] = jnp.zeros(acc_ref.shape, F32)
    for e in range(n_exp):
        for ch in range(N_CHUNK):
            @pl.when(used(e, ch))
            def _(e=e, ch=ch):
                _segment_copy(ys_ref, buf, sems, base_ref, tile, n_exp, e, ch, False).wait()
                cols = (ch * MOE_CHUNK + lax.broadcasted_iota(jnp.int32, (1, MOE_CHUNK), 1)).astype(F32)
                pick = jnp.where(rank[:, e:e + 1] == cols, sel[:, e:e + 1], 0.0).astype(BF16)
                acc_ref[...] += comb[:, e:e + 1] * _dot(pick, buf[e * N_CHUNK + ch])
    o_ref[...] = _layer_norm(alpha * x_ref[...] + acc_ref[...], g_ref[...], b_ref[...])


def _moe_ln(x, w_router, wg, wu, wd, g, b, alpha):
    n_valid, d = x.shape
    n_exp, _, dff = wg.shape
    n = _round_up(n_valid, MOE_TILE)
    n_tiles = n // MOE_TILE
    x = jnp.pad(x, ((0, n - n_valid), (0, 0)))
    comb, cnt = _router(x, w_router, n_valid)

    cnt = cnt[:, 0, :n_exp].astype(jnp.int32)
    seg = (cnt + SEG_ALIGN - 1) // SEG_ALIGN * SEG_ALIGN
    used = jnp.sum(seg, axis=0)
    region = (used + MOE_CHUNK + MOE_ROWS - 1) // MOE_ROWS * MOE_ROWS
    region_end = jnp.cumsum(region)
    seg_base = (region_end - region)[None, :] + jnp.cumsum(seg, axis=0) - seg
    n_rows = _round_up(TOP_K * n + n_tiles * n_exp * (SEG_ALIGN - 1) + n_exp * (MOE_CHUNK + MOE_ROWS), MOE_ROWS)
    n_rt = n_rows // MOE_ROWS
    tile_row0 = jnp.arange(n_rt, dtype=jnp.int32) * MOE_ROWS
    tile_exp = jnp.minimum(jnp.sum(region_end[None, :] <= tile_row0[:, None], axis=1), n_exp - 1).astype(jnp.int32)
    tile_valid = (tile_row0 < (region_end - region + used)[tile_exp]).astype(jnp.int32)
    seg_base = seg_base.reshape(-1).astype(jnp.int32)
    cnt = cnt.reshape(-1)

    t = np.arange(MOE_TILE)
    upper = jnp.asarray(t[:, None] < t[None, :], BF16)
    lower = jnp.asarray(t[None, :] < t[:, None], BF16)
    tile_spec = lambda w: pl.BlockSpec((MOE_TILE, w), lambda i, *_: (i, 0))
    const = lambda a: pl.BlockSpec(a.shape, lambda i, *_: (0,) * a.ndim)
    staging = [pltpu.VMEM((n_exp * N_CHUNK, MOE_CHUNK, d), BF16), pltpu.SemaphoreType.DMA((n_exp * N_CHUNK,))]

    xs = pl.pallas_call(
        functools.partial(_dispatch_kernel, n_exp),
        grid_spec=pltpu.PrefetchScalarGridSpec(
            num_scalar_prefetch=2, grid=(n_tiles,),
            in_specs=[tile_spec(d), tile_spec(LANES), const(upper), pl.BlockSpec(memory_space=pl.ANY)],
            out_specs=pl.BlockSpec(memory_space=pl.ANY),
            scratch_shapes=staging),
        out_shape=jax.ShapeDtypeStruct((n_rows, d), BF16),
        input_output_aliases={5: 0},
        compiler_params=_cparams("arbitrary"),
    )(seg_base, cnt, x, comb, upper, jnp.zeros((n_rows, d), BF16))

    tf = _ff_tile(dff)
    ys = pl.pallas_call(
        _expert_ffn_kernel,
        grid_spec=pltpu.PrefetchScalarGridSpec(
            num_scalar_prefetch=2, grid=(n_rt, dff // tf),
            in_specs=[pl.BlockSpec((MOE_ROWS, d), lambda j, k, te, tv: (j, 0)),
                      pl.BlockSpec((1, d, tf), lambda j, k, te, tv: (te[j], 0, k)),
                      pl.BlockSpec((1, d, tf), lambda j, k, te, tv: (te[j], 0, k)),
                      pl.BlockSpec((1, tf, d), lambda j, k, te, tv: (te[j], k, 0))],
            out_specs=pl.BlockSpec((MOE_ROWS, d), lambda j, k, te, tv: (j, 0)),
            scratch_shapes=[pltpu.VMEM((MOE_ROWS, d), F32)]),
        out_shape=jax.ShapeDtypeStruct((n_rows, d), BF16),
        compiler_params=_cparams("parallel", "arbitrary"),
    )(tile_exp, tile_valid, xs, wg, wu, wd)

    out = pl.pallas_call(
        functools.partial(_combine_kernel, n_exp, alpha),
        grid_spec=pltpu.PrefetchScalarGridSpec(
            num_scalar_prefetch=2, grid=(n_tiles,),
            in_specs=[tile_spec(d), tile_spec(LANES), const(lower), pl.BlockSpec(memory_space=pl.ANY),
                      const(g), const(b)],
            out_specs=tile_spec(d),
            scratch_shapes=staging + [pltpu.VMEM((MOE_TILE, d), F32)]),
        out_shape=jax.ShapeDtypeStruct((n, d), F32),
        compiler_params=_cparams("arbitrary"),
    )(seg_base, cnt, x, comb, lower, ys, g, b)
    return out[:n_valid]


def _rope_chunk(v, cos, sa, sb):
    half = ROT_DIM // 2
    return v * cos + pltpu.roll(v, LANES - half, 1) * sa + pltpu.roll(v, half, 1) * sb


def _nsa_proj_kernel(x_ref, w_ref, cos_ref, sa_ref, sb_ref,
                     q_ref, kc_ref, vc_ref, ks_ref, vs_ref, kw_ref, vw_ref, gate_ref):
    y = _dot(x_ref[...].astype(BF16), w_ref[...])
    cos, sa, sb = cos_ref[...], sa_ref[...], sb_ref[...]
    scale = HEAD_DIM ** -0.5
    for j in range(Q_COLS // LANES):
        q_ref[:, j * LANES:(j + 1) * LANES] = (
            _rope_chunk(y[:, j * LANES:(j + 1) * LANES], cos, sa, sb) * scale).astype(BF16)
    base = Q_COLS
    for idx, (ref, roped) in enumerate(((kc_ref, False), (vc_ref, False), (ks_ref, True),
                                        (vs_ref, False), (kw_ref, True), (vw_ref, False))):
        for j in range(KV_COLS // LANES):
            lo = base + idx * KV_COLS + j * LANES
            v = y[:, lo:lo + LANES]
            ref[:, j * LANES:(j + 1) * LANES] = _rope_chunk(v, cos, sa, sb) if roped else v
    gate_ref[...] = _sigmoid(y[:, base + 6 * KV_COLS:base + 6 * KV_COLS + LANES])


def _rope_tables(pos):
    half = ROT_DIM // 2
    inv_freq = ROPE_THETA ** (-jnp.arange(half, dtype=F32) / half)
    ang = pos.astype(F32)[:, None] * inv_freq[None, :]
    cos, sin = jnp.cos(ang), jnp.sin(ang)
    n = pos.shape[0]
    rest = HEAD_DIM - ROT_DIM
    one = jnp.ones((n, rest), F32)
    zero = jnp.zeros((n, rest), F32)
    zh = jnp.zeros((n, half), F32)
    c = jnp.concatenate([cos, cos, one], axis=1)
    sa = jnp.concatenate([-sin, zh, zero], axis=1)
    sb = jnp.concatenate([zh, sin, zero], axis=1)
    rep = LANES // HEAD_DIM
    return tuple(jnp.tile(t, (1, rep)) for t in (c, sa, sb))


def _nsa_proj(x, w_pad, tables, n_tab_blocks):
    n, d = x.shape
    tm = tables[0].shape[0] // n_tab_blocks
    assert n % tm == 0
    row = lambda i: (i, 0)
    tab = pl.BlockSpec((tm, LANES), lambda i: (i % n_tab_blocks, 0))
    kv_spec = pl.BlockSpec((tm, KV_COLS), row)
    kv_shape = jax.ShapeDtypeStruct((n, KV_COLS), F32)
    return pl.pallas_call(
        _nsa_proj_kernel,
        grid=(n // tm,),
        in_specs=[pl.BlockSpec((tm, d), row), pl.BlockSpec(w_pad.shape, lambda i: (0, 0)), tab, tab, tab],
        out_specs=[pl.BlockSpec((tm, Q_COLS), row)] + [kv_spec] * 6 + [pl.BlockSpec((tm, LANES), row)],
        out_shape=[jax.ShapeDtypeStruct((n, Q_COLS), BF16)] + [kv_shape] * 6
                  + [jax.ShapeDtypeStruct((n, LANES), F32)],
        compiler_params=_cparams("parallel"),
    )(x, w_pad, *tables)


def _compress_kernel(roped, g_ref, pea_ref, peb_ref, wa_ref, wb_ref, w2_ref, cos_ref, sa_ref, sb_ref, o_ref):
    grp = g_ref[0]
    rows = grp.shape[0]
    a = _dot((grp + pea_ref[...]).astype(BF16), wa_ref[...])
    bm = _dot((grp + peb_ref[...]).astype(BF16), wb_ref[...])
    h = a + pltpu.roll(bm, rows - 1, 0)
    out = _dot(_silu(h).astype(BF16), w2_ref[...])
    if roped:
        cos, sa, sb = cos_ref[...], sa_ref[...], sb_ref[...]
        for j in range(KV_COLS // LANES):
            o_ref[0, :, j * LANES:(j + 1) * LANES] = _rope_chunk(out[:, j * LANES:(j + 1) * LANES], cos, sa, sb)
    else:
        o_ref[0] = out


def _compress(groups, pe, w1, w2, roped):
    bsz, ng, gw = groups.shape
    assert CMP_LEN == 2 * CMP_STRIDE and gw == CMP_STRIDE * KV_COLS
    pea, peb, wa, wb, w2b = _compress_weights(pe, w1, w2)
    c_end = jnp.arange(ng) * CMP_STRIDE + (CMP_LEN - 1)
    tables = _rope_tables(c_end)
    full = lambda b: (0, 0)
    tab = pl.BlockSpec((ng, LANES), full)
    return pl.pallas_call(
        functools.partial(_compress_kernel, roped),
        grid=(bsz,),
        in_specs=[pl.BlockSpec((1, ng, gw), lambda b: (b, 0, 0)),
                  pl.BlockSpec((1, gw), full), pl.BlockSpec((1, gw), full),
                  pl.BlockSpec(wa.shape, full), pl.BlockSpec(wb.shape, full), pl.BlockSpec(w2b.shape, full),
                  tab, tab, tab],
        out_specs=pl.BlockSpec((1, ng, KV_COLS), lambda b: (b, 0, 0)),
        out_shape=jax.ShapeDtypeStruct((bsz, ng, KV_COLS), F32),
        compiler_params=_cparams("parallel"),
    )(groups, pea, peb, wa, wb, w2b, *tables)


def _softmax_cols(s, mask, col_valid=None):
    s = jnp.where(mask, s, NEG)
    p = jnp.exp(s - jnp.max(s, axis=0, keepdims=True))
    r = 1.0 / jnp.maximum(jnp.sum(p, axis=0, keepdims=True), 1e-30)
    return p * (r if col_valid is None else jnp.where(col_valid, r, 0.0))


def _attn_kernel(n_cmp, n_blk, qT_ref, kc_ref, vcT_ref, oh_ref, ks_ref, vsT_ref, kw_ref, vwT_ref,
                 gate_ref, ovT_ref, o_ref):
    q0 = pl.program_id(2) * Q_TILE
    qT = qT_ref[0, 0, 0]
    cols = qT.shape[1]
    qpos = q0 + lax.broadcasted_iota(jnp.int32, (1, cols), 1) % Q_TILE
    nb_pad = ovT_ref.shape[0]

    kc = kc_ref[0, 0]
    ncp = kc.shape[0]
    cidx = lax.broadcasted_iota(jnp.int32, (ncp, 1), 0)
    c_end = jnp.where(cidx < n_cmp, cidx * CMP_STRIDE + (CMP_LEN - 1), jnp.iinfo(jnp.int32).max)
    p_c = _softmax_cols(_dot(kc, qT), c_end <= qpos, qpos >= CMP_LEN - 1)
    o_c = _dot(vcT_ref[0, 0], p_c.astype(BF16))

    p_sum = p_c[:, 0:Q_TILE]
    for g in range(1, GROUP):
        p_sum = p_sum + p_c[:, g * Q_TILE:(g + 1) * Q_TILE]
    p_hi, p_lo = _split_bf16(p_sum)
    imp = _dot(ovT_ref[...], p_hi) + _dot(ovT_ref[...], p_lo)

    blk = lax.broadcasted_iota(jnp.int32, (nb_pad, 1), 0)
    qp = qpos[:, 0:Q_TILE]
    lag = qp // SEL_BLOCK - blk
    forced = (blk == 0) | ((lag >= 0) & (lag < N_LOCAL_SEL))
    score = jnp.where(forced, BIG, jnp.where(blk * SEL_BLOCK <= qp, imp, -BIG))
    score = jnp.where(blk < n_blk, score, -jnp.inf)
    bias = jnp.full(score.shape, NEG, F32)
    for _ in range(min(N_SELECT, n_blk)):
        m = jnp.max(score, axis=0, keepdims=True)
        j = jnp.min(jnp.where(score == m, blk, nb_pad), axis=0, keepdims=True)
        hit = blk == j
        bias = jnp.where(hit, 0.0, bias)
        score = jnp.where(hit, -jnp.inf, score)
    bias = bias.astype(BF16)
    q_aug = jnp.concatenate([jnp.concatenate([bias] * GROUP, axis=1), qT], axis=0)

    def scores(kb):
        k0 = pl.multiple_of(kb * KEY_TILE, KEY_TILE)
        keys = jnp.concatenate([oh_ref[pl.ds(k0, KEY_TILE), :], ks_ref[0, 0, pl.ds(k0, KEY_TILE), :]], axis=1)
        kpos = k0 + lax.broadcasted_iota(jnp.int32, (KEY_TILE, 1), 0)
        return jnp.where(kpos <= qpos, _dot(keys, q_aug), NEG)

    def accumulate(kb, s, m, l, acc):
        k0 = pl.multiple_of(kb * KEY_TILE, KEY_TILE)
        m_new = jnp.maximum(m, jnp.max(s, axis=0, keepdims=True))
        alpha = jnp.exp(m - m_new)
        p = jnp.exp(s - m_new)
        l = alpha * l + jnp.sum(p, axis=0, keepdims=True)
        acc = alpha * acc + _dot(vsT_ref[0, 0, :, pl.ds(k0, KEY_TILE)], p.astype(BF16))
        return m_new, l, acc

    def body(kb, carry):
        s, m, l, acc = carry
        return (scores(kb + 1),) + accumulate(kb, s, m, l, acc)

    n_full = q0 // KEY_TILE
    init = (scores(0), jnp.full((1, cols), NEG, F32), jnp.zeros((1, cols), F32),
            jnp.zeros((HEAD_DIM, cols), F32))
    s_last, m_s, l_s, acc_s = lax.fori_loop(0, n_full, body, init)
    _, l_s, acc_s = accumulate(n_full, s_last, m_s, l_s, acc_s)
    o_s = acc_s * (1.0 / jnp.maximum(l_s, 1e-30))

    wlen = WINDOW + Q_TILE
    w0 = pl.multiple_of(jnp.maximum(q0 - WINDOW, 0), Q_TILE)
    wpos = w0 + lax.broadcasted_iota(jnp.int32, (wlen, 1), 0)
    in_window = jnp.abs((2 * qpos - (WINDOW - 1)) - 2 * wpos) <= WINDOW - 1
    p_w = _softmax_cols(_dot(kw_ref[0, 0, pl.ds(w0, wlen), :], qT), in_window)
    o_w = _dot(vwT_ref[0, 0, :, pl.ds(w0, wlen)], p_w.astype(BF16))

    gate = gate_ref[0, 0, 0]
    o_ref[0, 0, 0] = (gate[0:1] * o_c + gate[1:2] * o_s + gate[2:3] * o_w).astype(BF16)


def _overlap_t(n_cmp, n_cmp_pad, n_blk, nb_pad):
    cs = np.arange(n_cmp_pad)[None, :] * CMP_STRIDE
    ss = np.arange(nb_pad)[:, None] * SEL_BLOCK
    ov = np.maximum(np.minimum(cs + CMP_LEN, ss + SEL_BLOCK) - np.maximum(cs, ss), 0) / CMP_LEN
    ov = ov * (np.arange(n_cmp_pad)[None, :] < n_cmp) * (np.arange(nb_pad)[:, None] < n_blk)
    return jnp.asarray(ov, BF16)


def _nsa_attention(q, kc, vc, ks, vs, kw, vw, gates):
    bsz, t, _ = q.shape
    n_qt = t // Q_TILE
    n_cmp = (t - CMP_LEN) // CMP_STRIDE + 1
    n_blk = -(-t // SEL_BLOCK)
    nb_pad = _round_up(n_blk, 16)
    ncp = kc.shape[1]
    cols = GROUP * Q_TILE
    assert t % KEY_TILE == 0 and t >= WINDOW + Q_TILE and n_cmp <= ncp

    heads = lambda a: a.astype(BF16).reshape(bsz, a.shape[1], N_KV_HEADS, HEAD_DIM).transpose(0, 2, 1, 3)
    heads_t = lambda a: a.astype(BF16).reshape(bsz, a.shape[1], N_KV_HEADS, HEAD_DIM).transpose(0, 2, 3, 1)
    qT = q.reshape(bsz, n_qt, Q_TILE, N_KV_HEADS, GROUP, HEAD_DIM).transpose(0, 3, 1, 5, 4, 2)
    qT = qT.reshape(bsz, N_KV_HEADS, n_qt, HEAD_DIM, cols)
    onehot = jnp.asarray(np.arange(t)[:, None] // SEL_BLOCK == np.arange(nb_pad)[None, :], BF16)
    gT = gates[..., :3 * N_HEADS].reshape(bsz, n_qt, Q_TILE, N_KV_HEADS, GROUP, 3).transpose(0, 3, 1, 5, 4, 2)
    gT = jnp.pad(gT.reshape(bsz, N_KV_HEADS, n_qt, 3, cols), ((0, 0), (0, 0), (0, 0), (0, 5), (0, 0)))
    ovT = _overlap_t(n_cmp, ncp, n_blk, nb_pad)

    per_head = lambda *shape: pl.BlockSpec((1, 1) + shape, lambda b, h, i: (b, h, 0, 0))
    per_tile = lambda *shape: pl.BlockSpec((1, 1, 1) + shape, lambda b, h, i: (b, h, i, 0, 0))
    const = lambda a: pl.BlockSpec(a.shape, lambda b, h, i: (0, 0))
    oT = pl.pallas_call(
        functools.partial(_attn_kernel, n_cmp, n_blk),
        grid=(bsz, N_KV_HEADS, n_qt),
        in_specs=[per_tile(HEAD_DIM, cols),
                  per_head(ncp, HEAD_DIM), per_head(HEAD_DIM, ncp),
                  const(onehot), per_head(t, HEAD_DIM), per_head(HEAD_DIM, t),
                  per_head(t, HEAD_DIM), per_head(HEAD_DIM, t),
                  per_tile(8, cols), const(ovT)],
        out_specs=per_tile(HEAD_DIM, cols),
        out_shape=jax.ShapeDtypeStruct((bsz, N_KV_HEADS, n_qt, HEAD_DIM, cols), BF16),
        compiler_params=_cparams("parallel", "parallel", "arbitrary"),
    )(qT, heads(kc), heads_t(vc), onehot, heads(ks), heads_t(vs), heads(kw), heads_t(vw), gT, ovT)
    o = oT.reshape(bsz, N_KV_HEADS, n_qt, HEAD_DIM, GROUP, Q_TILE).transpose(0, 2, 5, 1, 4, 3)
    return o.reshape(bsz, t, Q_COLS)


def _paged_pipeline(pt_ref, pools, bufs, sems, pages_per_chunk, extra_pages, window):
    seq, chunk = pl.program_id(0), pl.program_id(1)
    n_chunk = pl.num_programs(1)
    step = seq * n_chunk + chunk
    total = pl.num_programs(0) * n_chunk
    slot = step % 2
    last_page = n_chunk * pages_per_chunk - 1
    n_copy = pages_per_chunk + extra_pages

    def start(sq, ck, sl):
        def body(j, carry):
            page = pt_ref[sq, jnp.minimum(ck * pages_per_chunk + j, last_page)]
            for i, (pool, buf) in enumerate(zip(pools, bufs)):
                pltpu.make_async_copy(pool.at[page], window(buf, sl, j), sems.at[i, sl]).start()
            return carry
        lax.fori_loop(0, n_copy, body, 0)

    @pl.when(step == 0)
    def _():
        start(seq, chunk, slot)

    @pl.when(step + 1 < total)
    def _():
        nxt = step + 1
        start(nxt // n_chunk, nxt % n_chunk, 1 - slot)

    def wait_body(j, carry):
        for i, (pool, buf) in enumerate(zip(pools, bufs)):
            pltpu.make_async_copy(pool.at[0], window(buf, slot, j), sems.at[i, slot]).wait()
        return carry
    lax.fori_loop(0, n_copy, wait_body, 0)
    return slot


def _compress_paged_kernel(layer, pages_per_chunk, n_cmp, pt_ref, poolk_ref, poolv_ref,
                           pek_ref, w1k_ref, w2k_ref, pev_ref, w1v_ref, w2v_ref,
                           cos_ref, sa_ref, sb_ref, ok_ref, ov_ref, kbuf, vbuf, sems, xs_ref):
    page = poolk_ref.shape[-1]
    rows = pages_per_chunk * page // CMP_STRIDE
    window = lambda buf, sl, j: buf.at[sl, :, :, pl.ds(pl.multiple_of(j * page, page), page)]
    slot = _paged_pipeline(pt_ref, (poolk_ref.at[layer], poolv_ref.at[layer]), (kbuf, vbuf), sems,
                           pages_per_chunk, 1, window)
    row0 = pl.program_id(1) * rows
    valid = row0 + lax.broadcasted_iota(jnp.int32, (rows, 1), 0) < n_cmp

    def mlp(buf, pe_ref, w1_ref, w2_ref):
        n_half = xs_ref.shape[0]
        x_t = buf[slot].reshape(KV_COLS, xs_ref.shape[1]).T
        for j in range(n_half):
            xs_ref[j] = x_t[:, j * LANES:(j + 1) * LANES]
        window_rows = jnp.concatenate(
            [xs_ref[j, pl.ds(l, rows, stride=CMP_STRIDE), :] for l in range(CMP_LEN) for j in range(n_half)],
            axis=1)
        h = _dot((window_rows + pe_ref[...]).astype(BF16), w1_ref[...])
        return jnp.where(valid, _dot(_silu(h).astype(BF16), w2_ref[...]), 0.0)

    outk = mlp(kbuf, pek_ref, w1k_ref, w2k_ref)
    cos, sa, sb = cos_ref[...], sa_ref[...], sb_ref[...]
    for j in range(KV_COLS // LANES):
        ok_ref[0, :, j * LANES:(j + 1) * LANES] = _rope_chunk(
            outk[:, j * LANES:(j + 1) * LANES], cos, sa, sb).astype(BF16)
    ov_ref[0] = mlp(vbuf, pev_ref, w1v_ref, w2v_ref).astype(BF16)


def _compress_weights(pe, w1, w2):
    eye = jnp.eye(N_KV_HEADS, dtype=F32)
    blockdiag = lambda w: jnp.einsum('lde,hg->lhdge', w, eye).reshape(-1, KV_COLS).astype(BF16)
    w2b = jnp.einsum('de,hg->hdge', w2, eye).reshape(KV_COLS, KV_COLS).astype(BF16)
    pe_row = lambda p: jnp.tile(p[:, None, :], (1, N_KV_HEADS, 1)).reshape(1, -1)
    return (pe_row(pe[:CMP_STRIDE]), pe_row(pe[CMP_STRIDE:]),
            blockdiag(w1[:CMP_STRIDE]), blockdiag(w1[CMP_STRIDE:]), w2b)


def _compress_paged(pool_k, pool_v, layer, page_table, pe, w1, w2, n_cmp):
    bd, n_pages = page_table.shape
    page = pool_k.shape[2]
    n_chunk = 4 if n_pages % 4 == 0 else 1
    ppc = n_pages // n_chunk
    rows = ppc * page // CMP_STRIDE
    ng = n_pages * page // CMP_STRIDE
    npos = (ppc + 1) * page
    per_head_pages = lambda pool: pool.transpose(0, 1, 3, 4, 2)
    eye = jnp.eye(N_KV_HEADS, dtype=F32)

    def weights(i):
        w1bd = jnp.einsum('lde,hg->lhdge', w1[i], eye).reshape(CMP_LEN * KV_COLS, KV_COLS).astype(BF16)
        w2bd = jnp.einsum('de,hg->hdge', w2[i], eye).reshape(KV_COLS, KV_COLS).astype(BF16)
        return [jnp.tile(pe[i], (1, N_KV_HEADS)).reshape(1, CMP_LEN * KV_COLS), w1bd, w2bd]

    ws = weights(0) + weights(1)
    tables = _rope_tables(jnp.arange(ng) * CMP_STRIDE + (CMP_LEN - 1))
    wspecs = [pl.BlockSpec(w.shape, lambda b, c, pt, nd=w.ndim: (0,) * nd) for w in ws]
    tab = pl.BlockSpec((rows, LANES), lambda b, c, pt: (c, 0))
    out_spec = pl.BlockSpec((1, rows, KV_COLS), lambda b, c, pt: (b, c, 0))
    out_shape = jax.ShapeDtypeStruct((bd, ng, KV_COLS), BF16)
    pages = pltpu.VMEM((2, N_KV_HEADS, HEAD_DIM, npos), F32)
    return pl.pallas_call(
        functools.partial(_compress_paged_kernel, layer, ppc, n_cmp),
        grid_spec=pltpu.PrefetchScalarGridSpec(
            num_scalar_prefetch=1, grid=(bd, n_chunk),
            in_specs=[pl.BlockSpec(memory_space=pl.ANY), pl.BlockSpec(memory_space=pl.ANY)] + wspecs + [tab] * 3,
            out_specs=[out_spec, out_spec],
            scratch_shapes=[pages, pages, pltpu.SemaphoreType.DMA((2, 2)),
                            pltpu.VMEM((KV_COLS // LANES, npos, LANES), F32)]),
        out_shape=[out_shape, out_shape],
        compiler_params=_cparams("arbitrary", "arbitrary"),
    )(page_table, per_head_pages(pool_k), per_head_pages(pool_v), *ws, *tables)


NEW_PAD = 128
CHUNK_PAGES = 16


def _dot_nt(a, b):
    return lax.dot_general(a, b, (((1,), (1,)), ((), ())), preferred_element_type=F32)


def _softmax_rows2(s1, mask1, s2, mask2):
    s1, s2 = jnp.where(mask1, s1, NEG), jnp.where(mask2, s2, NEG)
    m = jnp.maximum(jnp.max(s1, axis=-1, keepdims=True), jnp.max(s2, axis=-1, keepdims=True))
    p1 = jnp.where(mask1, jnp.exp(s1 - m), 0.0)
    p2 = jnp.where(mask2, jnp.exp(s2 - m), 0.0)
    r = 1.0 / jnp.maximum(jnp.sum(p1, axis=-1, keepdims=True) + jnp.sum(p2, axis=-1, keepdims=True), 1e-30)
    return p1 * r, p2 * r


def _attn_sample_kernel(layer, s_new, past, n_cmp, n_blk, pt_ref, poolk_ref, poolv_ref, q_ref, kc_ref, vc_ref,
                        knew_ref, vnew_ref, kwb_ref, vwb_ref, kwn_ref, vwn_ref, gate_ref,
                        et_ref, ett_ref, ov_ref, gsum_ref, o_ref,
                        kbuf, vbuf, sems, m_ref, l_ref, acc_ref, bias_ref, part_ref):
    page = poolk_ref.shape[-1]
    window = lambda buf, sl, j: buf.at[sl, :, :, pl.ds(pl.multiple_of(j * page, page), page)]
    slot = _paged_pipeline(pt_ref, (poolk_ref.at[layer], poolv_ref.at[layer]), (kbuf, vbuf), sems,
                           CHUNK_PAGES, 0, window)
    chunk = pl.program_id(1)
    q = q_ref[0]
    n_rows = q.shape[0]
    qpos = past + lax.broadcasted_iota(jnp.int32, (n_rows, 1), 0) % s_new
    gate = gate_ref[0]

    @pl.when(chunk == 0)
    def _():
        ncp = kc_ref.shape[1]
        cidx = lax.broadcasted_iota(jnp.int32, (1, ncp), 1)
        cmask = (cidx * CMP_STRIDE + (CMP_LEN - 1) <= qpos) & (cidx < n_cmp)
        s_c = jnp.where(cmask, _dot_nt(q, kc_ref[0]), NEG)
        p_c = jnp.where(cmask, jnp.exp(s_c - jnp.max(s_c, axis=-1, keepdims=True)), 0.0)
        p_c = p_c * (1.0 / jnp.maximum(jnp.sum(p_c, axis=-1, keepdims=True), 1e-30))
        o_c = _dot(p_c.astype(BF16), vc_ref[0])
        p_hi, p_lo = _split_bf16(p_c)
        p_sum = _dot(gsum_ref[...], p_hi) + _dot(gsum_ref[...], p_lo)
        s_hi, s_lo = _split_bf16(p_sum)
        imp = _dot(s_hi, ov_ref[...]) + _dot(s_lo, ov_ref[...])
        nb_pad = imp.shape[1]
        blk = lax.broadcasted_iota(jnp.int32, (1, nb_pad), 1)
        lag = qpos // SEL_BLOCK - blk
        forced = (blk == 0) | ((lag >= 0) & (lag < N_LOCAL_SEL))
        score = jnp.where(forced, BIG, jnp.where(blk * SEL_BLOCK <= qpos, imp, -BIG))
        score = jnp.where(blk < n_blk, score, -jnp.inf)
        bias = jnp.full(score.shape, NEG, F32)
        for _ in range(min(N_SELECT, n_blk)):
            m = jnp.max(score, axis=-1, keepdims=True)
            j = jnp.min(jnp.where(score == m, blk, nb_pad), axis=-1, keepdims=True)
            hit = blk == j
            bias = jnp.where(hit, 0.0, bias)
            score = jnp.where(hit, -jnp.inf, score)
        bias_ref[...] = bias
        w_buf = kwb_ref.shape[1]
        wpos = past - w_buf + lax.broadcasted_iota(jnp.int32, (1, w_buf), 1)
        npos = lax.broadcasted_iota(jnp.int32, (1, NEW_PAD), 1)
        p1, p2 = _softmax_rows2(
            _dot_nt(q, kwb_ref[0].astype(BF16)), (wpos <= qpos) & (wpos > qpos - WINDOW),
            _dot_nt(q, kwn_ref[0]), (past + npos <= qpos) & (past + npos > qpos - WINDOW) & (npos < s_new))
        o_w = _dot(p1.astype(BF16), vwb_ref[0].astype(BF16)) + _dot(p2.astype(BF16), vwn_ref[0])
        part_ref[...] = gate[:, 0:1] * o_c + gate[:, 2:3] * o_w
        m_ref[...] = jnp.full(m_ref.shape, NEG, F32)
        l_ref[...] = jnp.zeros(l_ref.shape, F32)
        acc_ref[...] = jnp.zeros(acc_ref.shape, F32)

    bias16 = bias_ref[...].astype(BF16)

    def flash(s, pv):
        m_new = jnp.maximum(m_ref[...], jnp.max(s, axis=-1, keepdims=True))
        alpha = jnp.exp(m_ref[...] - m_new)
        p = jnp.exp(s - m_new)
        l_ref[...] = alpha * l_ref[...] + jnp.sum(p, axis=-1, keepdims=True)
        acc_ref[...] = alpha * acc_ref[...] + pv(p.astype(BF16))
        m_ref[...] = m_new

    rph = n_rows // N_KV_HEADS
    own = lambda a, h: a[h * rph:(h + 1) * rph, h * HEAD_DIM:(h + 1) * HEAD_DIM]

    def pv_chunk(p):
        rows = []
        for h in range(N_KV_HEADS):
            o_h = _dot_nt(p[h * rph:(h + 1) * rph], vbuf[slot, h].astype(BF16))
            zero = jnp.zeros((rph, HEAD_DIM), F32)
            rows.append(jnp.concatenate([zero] * h + [o_h] + [zero] * (N_KV_HEADS - 1 - h), axis=1))
        return jnp.concatenate(rows, axis=0)

    s_past = jnp.concatenate([_dot(own(q, h), kbuf[slot, h].astype(BF16)) for h in range(N_KV_HEADS)], axis=0)
    flash(s_past + _dot(bias16, et_ref[...]), pv_chunk)

    @pl.when(chunk == pl.num_programs(1) - 1)
    def _():
        npos = lax.broadcasted_iota(jnp.int32, (1, NEW_PAD), 1)
        s_t = _dot_nt(q, knew_ref[0]) + _dot(bias16, ett_ref[...])
        flash(jnp.where((past + npos <= qpos) & (npos < s_new), s_t, NEG), lambda p: _dot(p, vnew_ref[0]))
        o_s = acc_ref[...] * (1.0 / jnp.maximum(l_ref[...], 1e-30))
        o_ref[0] = part_ref[...] + gate[:, 1:2] * o_s


def _attn_sample(qs, kc, vc, pool_k, pool_v, layer, page_table, k_new, v_new, kw_buf, vw_buf, kw_new, vw_new,
                 gates, *, s_new):
    bd, n_pages = page_table.shape
    page = pool_k.shape[2]
    past = n_pages * page
    w_buf = kw_buf.shape[1]
    n_rows = N_HEADS * s_new
    per_head_pages = lambda pool: pool.transpose(0, 1, 3, 4, 2)
    n_cmp = (past + s_new - CMP_LEN) // CMP_STRIDE + 1
    n_blk = -(-(past + s_new) // SEL_BLOCK)
    nb_pad = _round_up(n_blk, 2 * LANES)
    ncp = kc.shape[1]
    assert n_pages % CHUNK_PAGES == 0 and s_new <= NEW_PAD and n_cmp <= ncp and w_buf == min(WINDOW, past)
    ck = CHUNK_PAGES * page
    eye = jnp.eye(N_KV_HEADS, dtype=BF16)

    q5 = qs.reshape(bd, s_new, N_KV_HEADS, GROUP, HEAD_DIM).transpose(0, 2, 3, 1, 4)
    qbig = jnp.einsum('bkgsd,kj->bkgsjd', q5, eye).reshape(bd, n_rows, KV_COLS)
    g5 = gates[:, :3 * N_HEADS].reshape(bd, s_new, N_KV_HEADS, GROUP, 3).transpose(0, 2, 3, 1, 4)
    gate = jnp.pad(g5.reshape(bd, n_rows, 3), ((0, 0), (0, 0), (0, LANES - 3)))
    newrows = lambda a: jnp.pad(a.reshape(bd, s_new, KV_COLS), ((0, 0), (0, NEW_PAD - s_new), (0, 0))).astype(BF16)
    flat = lambda a: a.reshape(a.shape[0], a.shape[1], KV_COLS)

    keyblk = np.arange(past + NEW_PAD)[None, :] // SEL_BLOCK
    et = jnp.asarray(keyblk == np.arange(nb_pad)[:, None], BF16)
    cs = np.arange(ncp)[:, None] * CMP_STRIDE
    ss = np.arange(nb_pad)[None, :] * SEL_BLOCK
    ov = np.maximum(np.minimum(cs + CMP_LEN, ss + SEL_BLOCK) - np.maximum(cs, ss), 0) / CMP_LEN
    ov = jnp.asarray(ov * (np.arange(ncp)[:, None] < n_cmp) * (np.arange(nb_pad)[None, :] < n_blk), BF16)
    r = np.arange(n_rows)
    same = (r[:, None] // (GROUP * s_new) == r[None, :] // (GROUP * s_new)) & (r[:, None] % s_new == r[None, :] % s_new)
    gsum = jnp.asarray(same, BF16)

    per_seq = lambda a: pl.BlockSpec((1,) + a.shape[1:], lambda b, c, pt: (b, 0, 0))
    const = lambda a: pl.BlockSpec(a.shape, lambda b, c, pt: (0, 0))
    operands = [qbig, kc, vc, newrows(k_new), newrows(v_new), flat(kw_buf), flat(vw_buf),
                newrows(kw_new), newrows(vw_new), gate]
    o_big = pl.pallas_call(
        functools.partial(_attn_sample_kernel, layer, s_new, past, n_cmp, n_blk),
        grid_spec=pltpu.PrefetchScalarGridSpec(
            num_scalar_prefetch=1, grid=(bd, n_pages // CHUNK_PAGES),
            in_specs=[pl.BlockSpec(memory_space=pl.ANY), pl.BlockSpec(memory_space=pl.ANY)]
                     + [per_seq(a) for a in operands]
                     + [pl.BlockSpec((nb_pad, ck), lambda b, c, pt: (0, c)),
                        pl.BlockSpec((nb_pad, NEW_PAD), lambda b, c, pt: (0, past // NEW_PAD)),
                        const(ov), const(gsum)],
            out_specs=pl.BlockSpec((1, n_rows, KV_COLS), lambda b, c, pt: (b, 0, 0)),
            scratch_shapes=[pltpu.VMEM((2, N_KV_HEADS, HEAD_DIM, ck), F32),
                            pltpu.VMEM((2, N_KV_HEADS, HEAD_DIM, ck), F32),
                            pltpu.SemaphoreType.DMA((2, 2)),
                            pltpu.VMEM((n_rows, 1), F32), pltpu.VMEM((n_rows, 1), F32),
                            pltpu.VMEM((n_rows, KV_COLS), F32), pltpu.VMEM((n_rows, nb_pad), F32),
                            pltpu.VMEM((n_rows, KV_COLS), F32)]),
        out_shape=jax.ShapeDtypeStruct((bd, n_rows, KV_COLS), F32),
        compiler_params=_cparams("arbitrary", "arbitrary"),
    )(page_table, per_head_pages(pool_k), per_head_pages(pool_v), *operands, et, et, ov, gsum)
    o6 = o_big.reshape(bd, N_KV_HEADS, GROUP, s_new, N_KV_HEADS, HEAD_DIM)
    own = jnp.eye(N_KV_HEADS, dtype=F32)[None, :, None, None, :, None]
    o = jnp.sum(o6 * own, axis=4).transpose(0, 3, 1, 2, 4)
    return o.reshape(bd * s_new, Q_COLS).astype(BF16)


def _proj_ln_kernel(alpha, a_ref, w_ref, x_ref, g_ref, b_ref, o_ref):
    h = _dot(a_ref[...], w_ref[...])
    o_ref[...] = _layer_norm(alpha * x_ref[...] + h, g_ref[...], b_ref[...])


def _proj_ln(a, w, x, g, b, alpha):
    n, d = x.shape
    tm = _row_tile(n, 512)
    vec = pl.BlockSpec((1, d), lambda i: (0, 0))
    return pl.pallas_call(
        functools.partial(_proj_ln_kernel, alpha),
        grid=(n // tm,),
        in_specs=[pl.BlockSpec((tm, a.shape[1]), lambda i: (i, 0)), pl.BlockSpec(w.shape, lambda i: (0, 0)),
                  pl.BlockSpec((tm, d), lambda i: (i, 0)), vec, vec],
        out_specs=pl.BlockSpec((tm, d), lambda i: (i, 0)),
        out_shape=jax.ShapeDtypeStruct((n, d), F32),
        compiler_params=_cparams("parallel"),
    )(a, w, x, g, b)


def _conv_layer(yp, ys, state, w_pw1, b_pw1, w_dw, b_dw, cg, cb, w_pw2, b_pw2, g, b, alpha):
    bsz, t, d = yp.shape
    bd, s, _ = ys.shape
    width = w_dw.shape[0]
    wa, wg = w_pw1[:, :d].astype(BF16), w_pw1[:, d:].astype(BF16)
    ba, bg = b_pw1[None, :d], b_pw1[None, d:]
    w2 = w_pw2.astype(BF16)
    row = lambda v: v[None, :]
    tail = (w_dw, row(b_dw), row(cg), row(cb), w2, row(b_pw2), row(g), row(b), alpha)

    u_p = _pw1_glu(yp.reshape(bsz * t, d), wa, wg, ba, bg).reshape(bsz, t, d)
    new_p = _conv2_prompt(u_p, yp, *tail)
    st_p = u_p[:, t - (width - 1):]

    u_s = _pw1_glu(ys.reshape(bd * s, d), wa, wg, ba, bg).reshape(bd, s, d)
    ext = jnp.concatenate([state, u_s], axis=1)
    rows = _round_up(s, 8)
    ext_pad = jnp.pad(ext, ((0, 0), (0, _round_up(width - 1 + rows, 8) - ext.shape[1]), (0, 0)))
    xs_pad = jnp.pad(ys, ((0, 0), (0, rows - s), (0, 0)))
    new_s = _conv2_sample(ext_pad, xs_pad, *tail)[:, :s]
    st_s = ext[:, s:]
    return new_p, new_s, st_p, st_s


def _nsa_layer(yp, ys, pool_kc, pool_vc, pool_ks, pool_vs, layer, kw_buf, vw_buf, page_table,
               w_in, pe, w1, w2, w_o, g, b, alpha):
    bsz, t, d = yp.shape
    bd, s, _ = ys.shape
    page = pool_kc.shape[2]
    past = page_table.shape[1] * page
    w_buf = kw_buf.shape[1]
    in_cols = w_in.shape[1]
    w_pad = jnp.pad(w_in, ((0, 0), (0, Q_COLS + 6 * KV_COLS + LANES - in_cols))).astype(BF16)
    w_o16 = w_o.astype(BF16)
    g, b = g[None, :], b[None, :]
    kv4 = lambda a, nb, nt: a.reshape(nb, nt, N_KV_HEADS, HEAD_DIM)

    tm = _row_tile(t, 256)
    tabs = _rope_tables(jnp.arange(t))
    q, k_c, v_c, k_s, v_s, k_w, v_w, gates = _nsa_proj(yp.reshape(bsz * t, d), w_pad, tabs, t // tm)
    b3 = lambda a: a.reshape(bsz, t, -1)
    grp = lambda a: a.reshape(bsz, t // CMP_STRIDE, CMP_STRIDE * KV_COLS)
    kc = _compress(grp(k_c), pe[0], w1[0], w2[0], True)
    vc = _compress(grp(v_c), pe[1], w1[1], w2[1], False)
    o = _nsa_attention(b3(q), kc, vc, b3(k_s), b3(v_s), b3(k_w), b3(v_w), b3(gates))
    new_p = _proj_ln(o.reshape(bsz * t, Q_COLS), w_o16, yp.reshape(bsz * t, d), g, b, alpha).reshape(bsz, t, d)
    win = lambda a: jnp.concatenate([jnp.zeros((bsz, w_buf, KV_COLS), F32), b3(a)], axis=1)[:, -w_buf:]
    st_p = tuple(kv4(a, bsz, t) for a in (k_c, v_c, k_s, v_s)) + (
        kv4(win(k_w), bsz, w_buf), kv4(win(v_w), bsz, w_buf))

    n_s = bd * s
    tabs_s = tuple(jnp.tile(tt, (bd, 1)) for tt in _rope_tables(past + jnp.arange(s)))
    qs, k_c2, v_c2, k_s2, v_s2, k_w2, v_w2, gates_s = _nsa_proj(ys.reshape(n_s, d), w_pad, tabs_s, 1)
    s3 = lambda a: a.reshape(bd, s, -1)
    n_cmp_s = (past + s - CMP_LEN) // CMP_STRIDE + 1
    assert n_cmp_s <= past // CMP_STRIDE
    kc_s, vc_s = _compress_paged(pool_kc, pool_vc, layer, page_table, pe, w1, w2, n_cmp_s)
    o_s = _attn_sample(qs, kc_s, vc_s, pool_ks, pool_vs, layer, page_table, k_s2, v_s2, kw_buf, vw_buf, k_w2, v_w2,
                       gates_s, s_new=s)
    kw_all = jnp.concatenate([kw_buf.reshape(bd, w_buf, KV_COLS), s3(k_w2)], axis=1)
    vw_all = jnp.concatenate([vw_buf.reshape(bd, w_buf, KV_COLS), s3(v_w2)], axis=1)
    new_s = _proj_ln(o_s, w_o16, ys.reshape(n_s, d), g, b, alpha).reshape(bd, s, d)
    st_s = tuple(kv4(a, bd, s) for a in (k_c2, v_c2, k_s2, v_s2)) + (
        kv4(kw_all[:, -w_buf:], bd, w_buf), kv4(vw_all[:, -w_buf:], bd, w_buf))
    return new_p, new_s, st_p, st_s


def kernel(x_prompt, x_sample, state_conv, cache_k_cmp, cache_v_cmp, cache_k_sel, cache_v_sel, state_k_win, state_v_win, page_table, ln_g, ln_b, conv_w_pw1, conv_b_pw1, conv_w_dw, conv_b_dw, conv_ln_g, conv_ln_b, conv_w_pw2, conv_b_pw2, nsa_w_in, nsa_cmp_pe, nsa_cmp_w1, nsa_cmp_w2, nsa_w_o, ffn_w_gate, ffn_w_up, ffn_w_down, moe_w_router, moe_w_gate, moe_w_up, moe_w_down):
    depth = ln_g.shape[0]
    alpha = (2 * depth) ** 0.25
    bsz, t, d = x_prompt.shape
    bd, s, _ = x_sample.shape
    yp, ys = x_prompt, x_sample
    conv_p, conv_s, nsa_p, nsa_s = [], [], [], []
    for i in range(depth):
        j = i // 2
        if i % 2 == 0:
            yp, ys, st_p, st_s = _conv_layer(
                yp, ys, state_conv[j], conv_w_pw1[j], conv_b_pw1[j], conv_w_dw[j], conv_b_dw[j],
                conv_ln_g[j], conv_ln_b[j], conv_w_pw2[j], conv_b_pw2[j], ln_g[i, 0], ln_b[i, 0], alpha)
            conv_p.append(st_p)
            conv_s.append(st_s)
        else:
            yp, ys, st_p, st_s = _nsa_layer(
                yp, ys, cache_k_cmp, cache_v_cmp, cache_k_sel, cache_v_sel, j,
                state_k_win[j], state_v_win[j], page_table,
                nsa_w_in[j], nsa_cmp_pe[j], nsa_cmp_w1[j], nsa_cmp_w2[j], nsa_w_o[j],
                ln_g[i, 0], ln_b[i, 0], alpha)
            nsa_p.append(st_p)
            nsa_s.append(st_s)
        g2, b2 = ln_g[i, 1][None, :], ln_b[i, 1][None, :]
        xp, xs = yp.reshape(bsz * t, d), ys.reshape(bd * s, d)
        if i % 2 == 0:
            fw = (ffn_w_gate[j].astype(BF16), ffn_w_up[j].astype(BF16), ffn_w_down[j].astype(BF16))
            xp = _ffn_ln(xp, *fw, g2, b2, alpha)
            xs = _ffn_ln(xs, *fw, g2, b2, alpha)
        else:
            mw = (moe_w_gate[j].astype(BF16), moe_w_up[j].astype(BF16), moe_w_down[j].astype(BF16))
            both = _moe_ln(jnp.concatenate([xp, xs], axis=0), moe_w_router[j], *mw, g2, b2, alpha)
            xp, xs = both[:bsz * t], both[bsz * t:]
        yp, ys = xp.reshape(bsz, t, d), xs.reshape(bd, s, d)
    stack = lambda parts: tuple(jnp.stack(a) for a in zip(*parts))
    return ((yp, ys, jnp.stack(conv_p)) + stack(nsa_p) + (jnp.stack(conv_s),) + stack(nsa_s))
```
